```python
import math
import jax
import jax.numpy as jnp
from jax import lax
import numpy as np

D_MODEL = 2048
BATCH = 4
SEQ = 4096
DEPTH = 4

EPS = 1e-5
MEM_LEN = 256
D_FF = 5632
N_EVEN = (DEPTH + 1) // 2
N_ODD = DEPTH // 2

SSD_HEADS = 32
SSD_HEAD_DIM = 64
SSD_D_INNER = SSD_HEADS * SSD_HEAD_DIM
SSD_GROUPS = 4
SSD_HEADS_PER_GROUP = SSD_HEADS // SSD_GROUPS
SSD_STATE = 128
SSD_CONV = 4
SSD_CHUNK = 128
SSD_XBC = SSD_D_INNER + 2 * SSD_GROUPS * SSD_STATE

DIL_PATTERNS = ((128, 1), (512, 4), (2048, 16))
DIL_N_GROUPS = len(DIL_PATTERNS)
DIL_HEADS = 8
DIL_HEAD_DIM = 128
DIL_WIDTH = DIL_HEADS * DIL_HEAD_DIM
DIL_BLOCK = 128

EVEN_Z_END = SSD_D_INNER
EVEN_XBC_END = EVEN_Z_END + SSD_XBC
EVEN_DT_END = EVEN_XBC_END + SSD_HEADS
EVEN_IN = EVEN_DT_END + 3 * DIL_N_GROUPS * DIL_WIDTH
EVEN_MIX = SSD_D_INNER + DIL_WIDTH

SWA_Q_HEADS = 32
SWA_KV_HEADS = 4
SWA_Q_PER_KV = SWA_Q_HEADS // SWA_KV_HEADS
SWA_HEAD_DIM = 64
SWA_WINDOW = 128
SWA_BLOCK = 128
ROPE_THETA = 150000.0
ODD_Q = SWA_Q_HEADS * SWA_HEAD_DIM
ODD_KV = SWA_KV_HEADS * SWA_HEAD_DIM
ODD_IN = ODD_Q + 2 * ODD_KV
ODD_MIX = ODD_Q

XA_HEADS = 4
XA_HEAD_DIM = 128
XA_WIDTH = XA_HEADS * XA_HEAD_DIM

kernel_name = 'hybrid_ssd_dilated_swasink_macaron_trunk'


def _rmsnorm(x, w):
    xf = x.astype(jnp.float32)
    xf = xf * lax.rsqrt(jnp.mean(xf * xf, axis=-1, keepdims=True) + EPS)
    return (xf * w.astype(jnp.float32)).astype(x.dtype)


def _swiglu(h, w1, w2):
    gate, up = jnp.split(h @ w1, 2, axis=-1)
    return (jax.nn.silu(gate) * up) @ w2


def _band_mask(n_blocks, blk, max_dist):
    qi = jnp.arange(blk)[:, None]
    kj = jnp.arange(2 * blk)[None, :]
    dist = blk + qi - kj
    valid = (dist >= 0) & (dist <= max_dist)
    not_first = (jnp.arange(n_blocks)[:, None, None] > 0) | (kj[None] >= blk)
    return valid[None] & not_first


def _with_prev_block(t, axis):
    return jnp.concatenate([jnp.roll(t, 1, axis=axis), t], axis=axis + 1)


def _causal_depthwise_conv(x, w, b):
    ch = x.shape[-1]
    y = lax.conv_general_dilated(x, w[:, None, :].astype(x.dtype), window_strides=(1,),
                                 padding=((SSD_CONV - 1, 0),),
                                 dimension_numbers=('NWC', 'WIO', 'NWC'),
                                 feature_group_count=ch)
    return y + b


def _ssd_chunked_scan(xh, dt, a, bm, cm):
    bsz, s, g, e, p = xh.shape
    L = SSD_CHUNK
    nc = s // L
    chunk = lambda t: t.reshape(bsz, nc, L, *t.shape[2:])
    da = chunk(dt * a)
    xdt = chunk(xh * dt[..., None])
    bc, cc = chunk(bm), chunk(cm)
    cs = jnp.cumsum(da, axis=2)
    causal = jnp.tril(jnp.ones((L, L), dtype=bool))
    seg = cs[:, :, :, None] - cs[:, :, None, :]
    decay = jnp.exp(jnp.where(causal[None, None, :, :, None, None], seg, -jnp.inf))
    cb = jnp.einsum('bclgn,bcsgn->bclsg', cc, bc)
    y_diag = jnp.einsum('bclsge,bcsgep->bclgep', cb[..., None] * decay, xdt)
    decay_to_end = jnp.exp(cs[:, :, -1:] - cs)
    states = jnp.einsum('bclgn,bclgep->bcgepn', bc, xdt * decay_to_end[..., None])
    chunk_decay = jnp.exp(cs[:, :, -1])

    def step(h, inp):
        st, dc = inp
        return h * dc[..., None, None] + st, h

    h0 = jnp.zeros_like(states[:, 0])
    _, h_in = lax.scan(step, h0, (jnp.moveaxis(states, 1, 0), jnp.moveaxis(chunk_decay, 1, 0)))
    h_in = jnp.moveaxis(h_in, 0, 1)
    y_off = jnp.einsum('bclgn,bcgepn->bclgep', cc, h_in) * jnp.exp(cs)[..., None]
    return (y_diag + y_off).reshape(bsz, s, g, e, p)


def _ssd_branch(z, xbc, dt_raw, conv_w, conv_b, dt_bias, a_log, d_skip, gate_norm):
    bsz, s, _ = z.shape
    f32 = jnp.float32
    xbc = jax.nn.silu(_causal_depthwise_conv(xbc, conv_w, conv_b)).astype(f32)
    xs, bm, cm = jnp.split(xbc, [SSD_D_INNER, SSD_D_INNER + SSD_GROUPS * SSD_STATE], axis=-1)
    xh = xs.reshape(bsz, s, SSD_GROUPS, SSD_HEADS_PER_GROUP, SSD_HEAD_DIM)
    bm = bm.reshape(bsz, s, SSD_GROUPS, SSD_STATE)
    cm = cm.reshape(bsz, s, SSD_GROUPS, SSD_STATE)
    dt = jax.nn.softplus((dt_raw + dt_bias).astype(f32)).reshape(bsz, s, SSD_GROUPS, SSD_HEADS_PER_GROUP)
    a = -jnp.exp(a_log.astype(f32)).reshape(SSD_GROUPS, SSD_HEADS_PER_GROUP)
    y = _ssd_chunked_scan(xh, dt, a, bm, cm)
    y = y + xh * d_skip.astype(f32).reshape(SSD_GROUPS, SSD_HEADS_PER_GROUP, 1)
    gsz = SSD_D_INNER // SSD_GROUPS
    y = y.reshape(bsz, s, SSD_GROUPS, gsz) * jax.nn.silu(z.astype(f32)).reshape(bsz, s, SSD_GROUPS, gsz)
    y = y * lax.rsqrt(jnp.mean(y * y, axis=-1, keepdims=True) + EPS)
    return (y.reshape(bsz, s, SSD_D_INNER) * gate_norm.astype(f32)).astype(z.dtype)


def _dilated_branch(q, k, v, dilation, n_keys):
    bsz, s, h, dh = q.shape
    blk = DIL_BLOCK
    sub = s // dilation
    nb = -(-sub // blk)
    pad = nb * blk - sub

    def to_blocks(t):
        t = t.reshape(bsz, sub, dilation, h, dh).transpose(0, 2, 1, 3, 4)
        t = jnp.pad(t, ((0, 0), (0, 0), (0, pad), (0, 0), (0, 0)))
        return t.reshape(bsz, dilation, nb, blk, h, dh)

    qb, kb, vb = to_blocks(q), to_blocks(k), to_blocks(v)
    kk = _with_prev_block(kb, 2)
    vv = _with_prev_block(vb, 2)
    sc = jnp.einsum('brnqhd,brnkhd->brnhqk', qb, kk).astype(jnp.float32) * (dh ** -0.5)
    mask = _band_mask(nb, blk, n_keys)
    sc = jnp.where(mask[None, None, :, None], sc, -jnp.inf)
    m = jnp.max(sc, axis=-1, keepdims=True)
    pr = jnp.exp(sc - m)
    den = jnp.sum(pr, axis=-1, keepdims=True)
    o = jnp.einsum('brnhqk,brnkhd->brnqhd', pr / den, vv.astype(jnp.float32))
    lse = jnp.swapaxes((m + jnp.log(den))[..., 0], -1, -2)

    def from_blocks(t):
        t = t.reshape(bsz, dilation, nb * blk, *t.shape[4:])[:, :, :sub]
        t = jnp.swapaxes(t, 1, 2)
        return t.reshape(bsz, s, *t.shape[3:])

    return from_blocks(o), from_blocks(lse)


def _dilated_mixer(qkv):
    bsz, s, _ = qkv.shape
    qkv = qkv.reshape(bsz, s, 3, DIL_N_GROUPS, DIL_HEADS, DIL_HEAD_DIM)
    outs, lses = [], []
    for g, (window, dilation) in enumerate(DIL_PATTERNS):
        o, l = _dilated_branch(qkv[:, :, 0, g], qkv[:, :, 1, g], qkv[:, :, 2, g], dilation, window // dilation)
        outs.append(o)
        lses.append(l)
    alpha = jax.nn.softmax(jnp.stack(lses), axis=0)
    o = jnp.sum(alpha[..., None] * jnp.stack(outs), axis=0)
    return o.reshape(bsz, s, DIL_WIDTH).astype(qkv.dtype)


def _even_mixer(h, w_in, conv_w, conv_b, dt_bias, a_log, d_skip, gate_norm, w_out):
    z, xbc, dt_raw, qkv = jnp.split(h @ w_in, [EVEN_Z_END, EVEN_XBC_END, EVEN_DT_END], axis=-1)
    y_a = _ssd_branch(z, xbc, dt_raw, conv_w, conv_b, dt_bias, a_log, d_skip, gate_norm)
    y_b = _dilated_mixer(qkv)
    return jnp.concatenate([y_a, y_b], axis=-1) @ w_out


def _rope(t, positions):
    half = SWA_HEAD_DIM // 2
    inv_freq = ROPE_THETA ** (-jnp.arange(half, dtype=jnp.float32) / half)
    ang = positions.astype(jnp.float32)[..., None] * inv_freq
    cos = jnp.cos(ang)[:, :, None, :]
    sin = jnp.sin(ang)[:, :, None, :]
    tf = t.astype(jnp.float32)
    t1, t2 = tf[..., :half], tf[..., half:]
    return jnp.concatenate([t1 * cos - t2 * sin, t2 * cos + t1 * sin], axis=-1).astype(t.dtype)


def _swa_sink_attention(q, k, v, sinks):
    bsz, s = q.shape[:2]
    blk = SWA_BLOCK
    nb = s // blk
    qb = q.reshape(bsz, nb, blk, SWA_KV_HEADS, SWA_Q_PER_KV, SWA_HEAD_DIM)
    kk = _with_prev_block(k.reshape(bsz, nb, blk, SWA_KV_HEADS, SWA_HEAD_DIM), 1)
    vv = _with_prev_block(v.reshape(bsz, nb, blk, SWA_KV_HEADS, SWA_HEAD_DIM), 1)
    sc = jnp.einsum('bnqgjd,bnkgd->bngjqk', qb, kk).astype(jnp.float32) * (SWA_HEAD_DIM ** -0.5)
    mask = _band_mask(nb, blk, SWA_WINDOW - 1)
    sc = jnp.where(mask[None, :, None, None], sc, -jnp.inf)
    sink = sinks.astype(jnp.float32).reshape(1, 1, SWA_KV_HEADS, SWA_Q_PER_KV, 1, 1)
    m = jnp.maximum(jnp.max(sc, axis=-1, keepdims=True), sink)
    pr = jnp.exp(sc - m)
    den = jnp.sum(pr, axis=-1, keepdims=True) + jnp.exp(sink - m)
    o = jnp.einsum('bngjqk,bnkgd->bnqgjd', pr / den, vv.astype(jnp.float32))
    return o.reshape(bsz, s, ODD_MIX)


def _odd_mixer(h, positions, w_in, b_in, sinks, w_out):
    bsz, s, _ = h.shape
    q, k, v = jnp.split(h @ w_in + b_in, [ODD_Q, ODD_Q + ODD_KV], axis=-1)
    q = _rope(q.reshape(bsz, s, SWA_Q_HEADS, SWA_HEAD_DIM), positions)
    k = _rope(k.reshape(bsz, s, SWA_KV_HEADS, SWA_HEAD_DIM), positions)
    v = v.reshape(bsz, s, SWA_KV_HEADS, SWA_HEAD_DIM)
    q = q.reshape(bsz, s, SWA_KV_HEADS, SWA_Q_PER_KV, SWA_HEAD_DIM)
    o = _swa_sink_attention(q, k, v, sinks)
    return o.astype(h.dtype) @ w_out


def _memory_cross_attention(h, mem_k, mem_v, w_q, w_o):
    bsz, s, _ = h.shape
    q = (h @ w_q).reshape(bsz, s, XA_HEADS, XA_HEAD_DIM)
    sc = jnp.einsum('bshd,bmhd->bhsm', q, mem_k).astype(jnp.float32) * (XA_HEAD_DIM ** -0.5)
    pr = jax.nn.softmax(sc, axis=-1)
    o = jnp.einsum('bhsm,bmhd->bshd', pr, mem_v.astype(jnp.float32)).astype(h.dtype)
    return o.reshape(bsz, s, XA_WIDTH) @ w_o


def setup_inputs(seed: int = 0) -> dict:
    key = jax.random.key(seed)
    ks = iter(jax.random.split(key, 40))
    f32 = jnp.float32

    def nrm(shape, scale):
        return jax.random.normal(next(ks), shape, f32) * scale

    def gain(shape):
        return 1.0 + nrm(shape, 0.02)

    x = nrm((BATCH, SEQ, D_MODEL), 1.0)
    mem = nrm((BATCH, MEM_LEN, D_MODEL), 1.0)
    positions = (jax.random.randint(next(ks), (BATCH, 1), 0, 1024, dtype=jnp.int32)
                 + jnp.arange(SEQ, dtype=jnp.int32)[None, :])
    ffn1_norm = gain((DEPTH, D_MODEL))
    ffn1_w1 = nrm((DEPTH, D_MODEL, 2 * D_FF), D_MODEL ** -0.5)
    ffn1_w2 = nrm((DEPTH, D_FF, D_MODEL), D_FF ** -0.5)
    mix_norm = gain((DEPTH, D_MODEL))
    even_w_in = nrm((N_EVEN, D_MODEL, EVEN_IN), D_MODEL ** -0.5)
    even_conv_w = nrm((N_EVEN, SSD_CONV, SSD_XBC), SSD_CONV ** -0.5)
    even_conv_b = nrm((N_EVEN, SSD_XBC), 0.02)
    dt0 = jnp.exp(jax.random.uniform(next(ks), (N_EVEN, SSD_HEADS), f32,
                                     minval=math.log(1e-3), maxval=math.log(1e-1)))
    even_dt_bias = dt0 + jnp.log(-jnp.expm1(-dt0))
    even_a_log = jnp.log(jax.random.uniform(next(ks), (N_EVEN, SSD_HEADS), f32, minval=1.0, maxval=16.0))
    even_d_skip = gain((N_EVEN, SSD_HEADS))
    even_gate_norm = gain((N_EVEN, SSD_D_INNER))
    even_w_out = nrm((N_EVEN, EVEN_MIX, D_MODEL), EVEN_MIX ** -0.5)
    odd_w_in = nrm((N_ODD, D_MODEL, ODD_IN), D_MODEL ** -0.5)
    odd_b_in = nrm((N_ODD, ODD_IN), 0.02)
    odd_sinks = nrm((N_ODD, SWA_Q_HEADS), 1.0)
    odd_w_out = nrm((N_ODD, ODD_MIX, D_MODEL), ODD_MIX ** -0.5)
    mem_norm = gain((D_MODEL,))
    mem_w_kv = nrm((D_MODEL, 2 * XA_WIDTH), D_MODEL ** -0.5)
    xa_norm = gain((DEPTH, D_MODEL))
    xa_w_q = nrm((DEPTH, D_MODEL, XA_WIDTH), D_MODEL ** -0.5)
    xa_w_o = nrm((DEPTH, XA_WIDTH, D_MODEL), XA_WIDTH ** -0.5)
    ffn2_norm = gain((DEPTH, D_MODEL))
    ffn2_w1 = nrm((DEPTH, D_MODEL, 2 * D_FF), D_MODEL ** -0.5)
    ffn2_w2 = nrm((DEPTH, D_FF, D_MODEL), D_FF ** -0.5)
    final_norm = gain((D_MODEL,))
    return {'x': x, 'mem': mem, 'positions': positions,
            'ffn1_norm': ffn1_norm, 'ffn1_w1': ffn1_w1, 'ffn1_w2': ffn1_w2,
            'mix_norm': mix_norm,
            'even_w_in': even_w_in, 'even_conv_w': even_conv_w, 'even_conv_b': even_conv_b,
            'even_dt_bias': even_dt_bias, 'even_a_log': even_a_log, 'even_d_skip': even_d_skip,
            'even_gate_norm': even_gate_norm, 'even_w_out': even_w_out,
            'odd_w_in': odd_w_in, 'odd_b_in': odd_b_in, 'odd_sinks': odd_sinks, 'odd_w_out': odd_w_out,
            'mem_norm': mem_norm, 'mem_w_kv': mem_w_kv,
            'xa_norm': xa_norm, 'xa_w_q': xa_w_q, 'xa_w_o': xa_w_o,
            'ffn2_norm': ffn2_norm, 'ffn2_w1': ffn2_w1, 'ffn2_w2': ffn2_w2,
            'final_norm': final_norm}


def reference(x, mem, positions, ffn1_norm, ffn1_w1, ffn1_w2, mix_norm,
              even_w_in, even_conv_w, even_conv_b, even_dt_bias, even_a_log, even_d_skip,
              even_gate_norm, even_w_out, odd_w_in, odd_b_in, odd_sinks, odd_w_out,
              mem_norm, mem_w_kv, xa_norm, xa_w_q, xa_w_o, ffn2_norm, ffn2_w1, ffn2_w2,
              final_norm):
    bsz = mem.shape[0]
    mem_k, mem_v = jnp.split(_rmsnorm(mem, mem_norm) @ mem_w_kv, 2, axis=-1)
    mem_k = mem_k.reshape(bsz, MEM_LEN, XA_HEADS, XA_HEAD_DIM)
    mem_v = mem_v.reshape(bsz, MEM_LEN, XA_HEADS, XA_HEAD_DIM)
    for i in range(DEPTH):
        j = i // 2
        x = x + 0.5 * _swiglu(_rmsnorm(x, ffn1_norm[i]), ffn1_w1[i], ffn1_w2[i])
        h = _rmsnorm(x, mix_norm[i])
        if i % 2 == 0:
            x = x + _even_mixer(h, even_w_in[j], even_conv_w[j], even_conv_b[j], even_dt_bias[j],
                                even_a_log[j], even_d_skip[j], even_gate_norm[j], even_w_out[j])
        else:
            x = x + _odd_mixer(h, positions, odd_w_in[j], odd_b_in[j], odd_sinks[j], odd_w_out[j])
        x = x + _memory_cross_attention(_rmsnorm(x, xa_norm[i]), mem_k, mem_v, xa_w_q[i], xa_w_o[i])
        x = x + 0.5 * _swiglu(_rmsnorm(x, ffn2_norm[i]), ffn2_w1[i], ffn2_w2[i])
    return _rmsnorm(x, final_norm)
```

```python
import functools
import math

import jax
import jax.numpy as jnp
from jax import lax
from jax.experimental import pallas as pl
from jax.experimental.pallas import tpu as pltpu

F32 = jnp.float32
BF16 = jnp.bfloat16

EPS = 1e-5
MEM_LEN = 256

SSD_HEADS = 32
SSD_HEAD_DIM = 64
SSD_D_INNER = SSD_HEADS * SSD_HEAD_DIM
SSD_GROUPS = 4
SSD_STATE = 128
SSD_CONV = 4
SSD_CHUNK = 128
SSD_XBC = SSD_D_INNER + 2 * SSD_GROUPS * SSD_STATE
SSD_GROUP_WIDTH = SSD_D_INNER // SSD_GROUPS

DIL_PATTERNS = ((128, 1), (512, 4), (2048, 16))
DIL_HEADS = 8
DIL_HEAD_DIM = 128
DIL_WIDTH = DIL_HEADS * DIL_HEAD_DIM

SWA_Q_HEADS = 32
SWA_KV_HEADS = 4
SWA_HEAD_DIM = 64
SWA_WINDOW = 128
ROPE_THETA = 150000.0
ODD_Q = SWA_Q_HEADS * SWA_HEAD_DIM
ODD_KV = SWA_KV_HEADS * SWA_HEAD_DIM

XA_HEADS = 4
XA_HEAD_DIM = 128
XA_WIDTH = XA_HEADS * XA_HEAD_DIM

LANES = 128
ATT_BLOCK = 128
VMEM_LIMIT = 56 * 1024 * 1024


def _cparams(*sem):
    return pltpu.CompilerParams(dimension_semantics=sem, vmem_limit_bytes=VMEM_LIMIT)


def _rms(x, gain):
    ms = jnp.mean(x * x, axis=-1, keepdims=True)
    return x * lax.rsqrt(ms + EPS) * gain


def _sigmoid(x):
    return 1.0 / (1.0 + jnp.exp(-x))


def _dot(a, b):
    return jnp.dot(a, b, preferred_element_type=F32)


def _dot_nt(a, b):
    return lax.dot_general(a, b, (((1,), (1,)), ((), ())), preferred_element_type=F32)


def _split_dot(v, e, parts):
    out = None
    rem = v
    for _ in range(parts):
        piece = rem.astype(BF16)
        term = _dot(piece, e)
        out = term if out is None else out + term
        rem = rem - piece.astype(F32)
    return out


def _rmsnorm_kernel(x_ref, g_ref, o_ref):
    o_ref[...] = _rms(x_ref[...], g_ref[...]).astype(o_ref.dtype)


def _rmsnorm(x, gain, out_dtype, tm=512):
    t, d = x.shape
    tm = min(tm, t)
    return pl.pallas_call(
        _rmsnorm_kernel,
        grid=(t // tm,),
        in_specs=[pl.BlockSpec((tm, d), lambda i: (i, 0)),
                  pl.BlockSpec((1, d), lambda i: (0, 0))],
        out_specs=pl.BlockSpec((tm, d), lambda i: (i, 0)),
        out_shape=jax.ShapeDtypeStruct((t, d), out_dtype),
        compiler_params=_cparams("parallel"),
        name="rmsnorm",
    )(x, gain.reshape(1, d))


def _mm_kernel(a_ref, w_ref, o_ref):
    o_ref[...] = _dot(a_ref[...], w_ref[...]).astype(o_ref.dtype)


def _matmul(a, w, out_dtype, tm=1024, tn=512, name="matmul"):
    t, k = a.shape
    n = w.shape[1]
    tm, tn = min(tm, t), min(tn, n)
    return pl.pallas_call(
        _mm_kernel,
        grid=(t // tm, n // tn),
        in_specs=[pl.BlockSpec((tm, k), lambda i, j: (i, 0)),
                  pl.BlockSpec((k, tn), lambda i, j: (0, j))],
        out_specs=pl.BlockSpec((tm, tn), lambda i, j: (i, j)),
        out_shape=jax.ShapeDtypeStruct((t, n), out_dtype),
        compiler_params=_cparams("parallel", "arbitrary"),
        name=name,
    )(a, w)


def _matmul_dilated_rows(h, w, bsz, seq, dil, tn=512):
    k = h.shape[1]
    n = w.shape[1]
    sub = seq // dil
    tm = min(1024, sub)
    tiles = sub // tm
    h2 = h.reshape(bsz * sub, dil * k)
    return pl.pallas_call(
        _mm_kernel,
        grid=(bsz, dil, tiles, n // tn),
        in_specs=[pl.BlockSpec((tm, k), lambda b, c, i, j: (b * tiles + i, c)),
                  pl.BlockSpec((k, tn), lambda b, c, i, j: (0, j))],
        out_specs=pl.BlockSpec((None, None, tm, tn), lambda b, c, i, j: (b, c, i, j)),
        out_shape=jax.ShapeDtypeStruct((bsz, dil, sub, n), BF16),
        compiler_params=_cparams("parallel", "parallel", "parallel", "arbitrary"),
        name="dilated_qkv_proj",
    )(h2, w)


def _mm_dt_kernel(a_ref, w_ref, wdt_ref, o_ref, dt_ref):
    a = a_ref[...]
    o_ref[...] = _dot(a, w_ref[...]).astype(o_ref.dtype)

    @pl.when(pl.program_id(1) == 0)
    def _():
        dt_ref[...] = _dot(a, wdt_ref[...])


def _matmul_with_dt(a, w, w_dt, tm=1024, tn=512):
    t, k = a.shape
    n = w.shape[1]
    return pl.pallas_call(
        _mm_dt_kernel,
        grid=(t // tm, n // tn),
        in_specs=[pl.BlockSpec((tm, k), lambda i, j: (i, 0)),
                  pl.BlockSpec((k, tn), lambda i, j: (0, j)),
                  pl.BlockSpec((k, LANES), lambda i, j: (0, 0))],
        out_specs=[pl.BlockSpec((tm, tn), lambda i, j: (i, j)),
                   pl.BlockSpec((tm, LANES), lambda i, j: (i, 0))],
        out_shape=[jax.ShapeDtypeStruct((t, n), BF16),
                   jax.ShapeDtypeStruct((t, LANES), F32)],
        compiler_params=_cparams("parallel", "arbitrary"),
        name="ssd_in_proj",
    )(a, w, w_dt)


def _mm_rope_kernel(a_ref, w_ref, b_ref, cos_ref, sin_ref, o_ref, *, rope_tiles):
    j = pl.program_id(1)
    acc = _dot(a_ref[...], w_ref[...]) + b_ref[...]
    tm, tn = acc.shape

    @pl.when(j < rope_tiles)
    def _():
        cos = cos_ref[...]
        sin = sin_ref[...]
        lane = lax.broadcasted_iota(jnp.int32, (tm, LANES), 1)
        first_half = (lane % SWA_HEAD_DIM) < (SWA_HEAD_DIM // 2)
        for ch in range(tn // LANES):
            t = acc[:, ch * LANES:(ch + 1) * LANES]
            partner = jnp.where(first_half,
                                pltpu.roll(t, LANES - SWA_HEAD_DIM // 2, axis=1),
                                pltpu.roll(t, SWA_HEAD_DIM // 2, axis=1))
            o_ref[:, ch * LANES:(ch + 1) * LANES] = (t * cos + partner * sin).astype(o_ref.dtype)

    @pl.when(j >= rope_tiles)
    def _():
        o_ref[...] = acc.astype(o_ref.dtype)


def _matmul_bias_rope(a, w, bias, cos, sin, rope_cols, tm=1024, tn=512):
    t, k = a.shape
    n = w.shape[1]
    return pl.pallas_call(
        functools.partial(_mm_rope_kernel, rope_tiles=rope_cols // tn),
        grid=(t // tm, n // tn),
        in_specs=[pl.BlockSpec((tm, k), lambda i, j: (i, 0)),
                  pl.BlockSpec((k, tn), lambda i, j: (0, j)),
                  pl.BlockSpec((1, tn), lambda i, j: (0, j)),
                  pl.BlockSpec((tm, LANES), lambda i, j: (i, 0)),
                  pl.BlockSpec((tm, LANES), lambda i, j: (i, 0))],
        out_specs=pl.BlockSpec((tm, tn), lambda i, j: (i, j)),
        out_shape=jax.ShapeDtypeStruct((t, n), BF16),
        compiler_params=_cparams("parallel", "arbitrary"),
        name="swa_in_proj",
    )(a, w, bias.reshape(1, n), cos, sin)


def _mm_res_kernel(a_ref, w_ref, r_ref, o_ref):
    o_ref[...] = r_ref[...] + _dot(a_ref[...], w_ref[...])


def _mm2_res_kernel(a1_ref, a2_ref, w1_ref, w2_ref, r_ref, o_ref):
    o_ref[...] = r_ref[...] + _dot(a1_ref[...], w1_ref[...]) + _dot(a2_ref[...], w2_ref[...])


def _matmul_residual(parts, res, tm=1024, tn=512):
    t, n = res.shape
    a_specs = [pl.BlockSpec((tm, a.shape[1]), lambda i, j: (i, 0)) for a, _ in parts]
    w_specs = [pl.BlockSpec((w.shape[0], tn), lambda i, j: (0, j)) for _, w in parts]
    body = _mm_res_kernel if len(parts) == 1 else _mm2_res_kernel
    return pl.pallas_call(
        body,
        grid=(t // tm, n // tn),
        in_specs=a_specs + w_specs + [pl.BlockSpec((tm, tn), lambda i, j: (i, j))],
        out_specs=pl.BlockSpec((tm, tn), lambda i, j: (i, j)),
        out_shape=jax.ShapeDtypeStruct((t, n), F32),
        compiler_params=_cparams("parallel", "arbitrary"),
        name="out_proj_residual",
    )(*[a for a, _ in parts], *[w for _, w in parts], res)


def _ffn_kernel(x_ref, g_ref, w1g_ref, w1u_ref, w2_ref, fg_ref, o_ref, h_scr, *, final_norm):
    k = pl.program_id(1)

    @pl.when(k == 0)
    def _():
        x = x_ref[...]
        h_scr[...] = _rms(x, g_ref[...]).astype(BF16)
        o_ref[...] = x

    h = h_scr[...]
    gate = _dot(h, w1g_ref[...])
    up = _dot(h, w1u_ref[...])
    act = (gate * _sigmoid(gate) * up * 0.5).astype(BF16)
    o_ref[...] += _dot(act, w2_ref[...])

    if final_norm:
        @pl.when(k == pl.num_programs(1) - 1)
        def _():
            o_ref[...] = _rms(o_ref[...], fg_ref[...])


def _ffn(x, gain, w1, w2, final_gain=None, tm=1024, tf=256):
    t, d = x.shape
    dff = w2.shape[0]
    nk = dff // tf
    fg = jnp.ones((d,), F32) if final_gain is None else final_gain
    return pl.pallas_call(
        functools.partial(_ffn_kernel, final_norm=final_gain is not None),
        grid=(t // tm, nk),
        in_specs=[pl.BlockSpec((tm, d), lambda i, k: (i, 0)),
                  pl.BlockSpec((1, d), lambda i, k: (0, 0)),
                  pl.BlockSpec((d, tf), lambda i, k: (0, k)),
                  pl.BlockSpec((d, tf), lambda i, k: (0, nk + k)),
                  pl.BlockSpec((tf, d), lambda i, k: (k, 0)),
                  pl.BlockSpec((1, d), lambda i, k: (0, 0))],
        out_specs=pl.BlockSpec((tm, d), lambda i, k: (i, 0)),
        out_shape=jax.ShapeDtypeStruct((t, d), F32),
        scratch_shapes=[pltpu.VMEM((tm, d), BF16)],
        compiler_params=_cparams("parallel", "arbitrary"),
        name="swiglu_ffn",
    )(x, gain.reshape(1, d), w1, w1, w2, fg.reshape(1, d))


def _xa_kernel(x_ref, g_ref, wq_ref, k_ref, v_ref, wo_ref, o_ref):
    x = x_ref[...]
    h = _rms(x, g_ref[...]).astype(BF16)
    q = _dot(h, wq_ref[...]).astype(BF16)
    scale = XA_HEAD_DIM ** -0.5
    outs = []
    for hd in range(XA_HEADS):
        sl = slice(hd * XA_HEAD_DIM, (hd + 1) * XA_HEAD_DIM)
        s = _dot_nt(q[:, sl], k_ref[:, sl]) * scale
        m = jnp.max(s, axis=-1, keepdims=True)
        p = jnp.exp(s - m)
        den = jnp.sum(p, axis=-1, keepdims=True)
        outs.append((_dot(p.astype(BF16), v_ref[:, sl]) / den).astype(BF16))
    o = jnp.concatenate(outs, axis=-1)
    o_ref[...] = x + _dot(o, wo_ref[...])


def _cross_attention(x, gain, w_q, mem_kv, w_o, seq, tm=512):
    t, d = x.shape
    per_batch = seq // tm
    return pl.pallas_call(
        _xa_kernel,
        grid=(t // tm,),
        in_specs=[pl.BlockSpec((tm, d), lambda i: (i, 0)),
                  pl.BlockSpec((1, d), lambda i: (0, 0)),
                  pl.BlockSpec((d, XA_WIDTH), lambda i: (0, 0)),
                  pl.BlockSpec((MEM_LEN, XA_WIDTH), lambda i: (i // per_batch, 0)),
                  pl.BlockSpec((MEM_LEN, XA_WIDTH), lambda i: (i // per_batch, 1)),
                  pl.BlockSpec((XA_WIDTH, d), lambda i: (0, 0))],
        out_specs=pl.BlockSpec((tm, d), lambda i: (i, 0)),
        out_shape=jax.ShapeDtypeStruct((t, d), F32),
        compiler_params=_cparams("parallel"),
        name="memory_cross_attention",
    )(x, gain.reshape(1, d), w_q, mem_kv, mem_kv, w_o)


def _ssd_kernel(z_ref, xs_ref, bc_ref, dt_ref, cw_ref, cb_ref, dtb_ref, alog_ref, dskip_ref,
                gn_ref, e_ref, o_ref, ext_scr, h_scr):
    L = SSD_CHUNK
    halo = 8
    c = pl.program_id(1)

    @pl.when(c == 0)
    def _():
        ext_scr[0:halo, :] = jnp.zeros((halo, SSD_XBC), F32)
        h_scr[...] = jnp.zeros(h_scr.shape, F32)

    ext_scr[halo:halo + L, 0:SSD_D_INNER] = xs_ref[...].astype(F32)
    ext_scr[halo:halo + L, SSD_D_INNER:SSD_XBC] = bc_ref[...].astype(F32)
    conv = cb_ref[...] + ext_scr[halo - 3:halo - 3 + L, :] * cw_ref[0:1, :]
    for tap in range(1, SSD_CONV):
        conv = conv + ext_scr[halo - 3 + tap:halo - 3 + tap + L, :] * cw_ref[tap:tap + 1, :]
    ext_scr[0:halo, :] = ext_scr[L:L + halo, :]
    xbc = conv * _sigmoid(conv)

    v = dt_ref[...] + dtb_ref[...]
    dt = jnp.maximum(v, 0.0) + jnp.log1p(jnp.exp(-jnp.abs(v)))
    da = dt * (-jnp.exp(alog_ref[...]))
    row = lax.broadcasted_iota(jnp.int32, (L, LANES), 0)
    col = lax.broadcasted_iota(jnp.int32, (L, LANES), 1)
    cs = da
    shift = 1
    while shift < L:
        cs = cs + jnp.where(row >= shift, pltpu.roll(cs, shift, axis=0), 0.0)
        shift *= 2
    exp_cs = jnp.exp(cs)
    decay_to_end = jnp.exp(cs[L - 1:L, :] - cs)
    cs_t = cs.T

    expanded = _split_dot(jnp.concatenate([dt, exp_cs, decay_to_end], axis=0), e_ref[...], 3)
    dt_x = expanded[0:L]
    exp_cs_x = expanded[L:2 * L]
    dte_x = expanded[2 * L:3 * L]

    causal = row >= col
    low_half = col < SSD_HEAD_DIM
    gw = SSD_GROUP_WIDTH
    for g in range(SSD_GROUPS):
        gs = slice(g * gw, (g + 1) * gw)
        b_f32 = xbc[:, SSD_D_INNER + g * SSD_STATE:SSD_D_INNER + (g + 1) * SSD_STATE]
        b_g = b_f32.astype(BF16)
        b_t = b_f32.T.astype(BF16)
        c_off = SSD_D_INNER + SSD_GROUPS * SSD_STATE
        c_g = xbc[:, c_off + g * SSD_STATE:c_off + (g + 1) * SSD_STATE].astype(BF16)
        x_g = xbc[:, gs]
        xdt = x_g * dt_x[:, gs]
        xdt_bf = xdt.astype(BF16)
        cb = _dot_nt(c_g, b_g)

        h_in = h_scr[g]
        y_off = _dot(c_g, h_in.astype(BF16)) * exp_cs_x[:, gs]
        new_state = _dot(b_t, (xdt * dte_x[:, gs]).astype(BF16))
        h_scr[g] = h_in * exp_cs_x[L - 1:L, gs] + new_state

        y_parts = []
        heads_per_group = SSD_HEADS // SSD_GROUPS
        for pair in range(heads_per_group // 2):
            x_pair = xdt_bf[:, pair * LANES:(pair + 1) * LANES]
            acc = None
            for par in range(2):
                hd = g * heads_per_group + 2 * pair + par
                seg = cs[:, hd:hd + 1] - cs_t[hd:hd + 1, :]
                decay = jnp.exp(jnp.where(causal, seg, -jnp.inf))
                m = (cb * decay).astype(BF16)
                rhs = jnp.where(low_half if par == 0 else ~low_half, x_pair, jnp.zeros_like(x_pair))
                term = _dot(m, rhs)
                acc = term if acc is None else acc + term
            y_parts.append(acc)
        y = jnp.concatenate(y_parts, axis=-1) + y_off + x_g * dskip_ref[:, gs]
        zg = z_ref[:, gs].astype(F32)
        y = y * (zg * _sigmoid(zg))
        y = y * lax.rsqrt(jnp.mean(y * y, axis=-1, keepdims=True) + EPS) * gn_ref[:, gs]
        o_ref[:, gs] = y.astype(o_ref.dtype)


def _ssd_branch(zx, dt_raw, conv_w, conv_b, dt_bias, a_log, d_skip, gate_norm, bsz, seq):
    t = zx.shape[0]
    L = SSD_CHUNK
    nc = seq // L
    pad = LANES - SSD_HEADS
    head_of_col = jnp.arange(SSD_D_INNER) // SSD_HEAD_DIM
    expand = (jnp.arange(LANES)[:, None] == head_of_col[None, :]).astype(BF16)
    row_map = lambda b, c: (b * nc + c, 0)
    const = lambda b, c: (0, 0)
    return pl.pallas_call(
        _ssd_kernel,
        grid=(bsz, nc),
        in_specs=[pl.BlockSpec((L, SSD_D_INNER), row_map),
                  pl.BlockSpec((L, SSD_D_INNER), lambda b, c: (b * nc + c, 1)),
                  pl.BlockSpec((L, 2 * SSD_GROUPS * SSD_STATE), lambda b, c: (b * nc + c, 4)),
                  pl.BlockSpec((L, LANES), row_map),
                  pl.BlockSpec((SSD_CONV, SSD_XBC), const),
                  pl.BlockSpec((1, SSD_XBC), const),
                  pl.BlockSpec((1, LANES), const),
                  pl.BlockSpec((1, LANES), const),
                  pl.BlockSpec((1, SSD_D_INNER), const),
                  pl.BlockSpec((1, SSD_D_INNER), const),
                  pl.BlockSpec((LANES, SSD_D_INNER), const)],
        out_specs=pl.BlockSpec((L, SSD_D_INNER), row_map),
        out_shape=jax.ShapeDtypeStruct((t, SSD_D_INNER), BF16),
        scratch_shapes=[pltpu.VMEM((L + 8, SSD_XBC), F32),
                        pltpu.VMEM((SSD_GROUPS, SSD_STATE, SSD_GROUP_WIDTH), F32)],
        compiler_params=_cparams("arbitrary", "arbitrary"),
        name="ssd_scan",
    )(zx, zx, zx, dt_raw, conv_w, conv_b.reshape(1, -1),
      jnp.pad(dt_bias, (0, pad)).reshape(1, LANES), jnp.pad(a_log, (0, pad)).reshape(1, LANES),
      jnp.repeat(d_skip, SSD_HEAD_DIM).reshape(1, -1), gate_norm.reshape(1, -1), expand)


def _band_masks(first_block, max_dist):
    qi = lax.broadcasted_iota(jnp.int32, (ATT_BLOCK, ATT_BLOCK), 0)
    kj = lax.broadcasted_iota(jnp.int32, (ATT_BLOCK, ATT_BLOCK), 1)
    no_prev = jnp.where(first_block, 2 * ATT_BLOCK, 0)
    prev_ok = kj >= qi + (ATT_BLOCK - max_dist) + no_prev
    cur_ok = kj <= qi
    return prev_ok, cur_ok


def _dil_attn_kernel(q_ref, kc_ref, vc_ref, kp_ref, vp_ref, o_ref, lse_ref, *, max_dist):
    prev_ok, cur_ok = _band_masks(pl.program_id(2) == 0, max_dist)
    lane = lax.broadcasted_iota(jnp.int32, (ATT_BLOCK, LANES), 1)
    scale = DIL_HEAD_DIM ** -0.5
    lse_tile = jnp.zeros((ATT_BLOCK, LANES), F32)
    for hd in range(DIL_HEADS):
        sl = slice(hd * DIL_HEAD_DIM, (hd + 1) * DIL_HEAD_DIM)
        q = q_ref[:, sl]
        s_p = jnp.where(prev_ok, _dot_nt(q, kp_ref[:, sl]) * scale, -jnp.inf)
        s_c = jnp.where(cur_ok, _dot_nt(q, kc_ref[:, sl]) * scale, -jnp.inf)
        m = jnp.maximum(jnp.max(s_p, axis=-1, keepdims=True), jnp.max(s_c, axis=-1, keepdims=True))
        p_p = jnp.exp(s_p - m)
        p_c = jnp.exp(s_c - m)
        den = jnp.sum(p_p, axis=-1, keepdims=True) + jnp.sum(p_c, axis=-1, keepdims=True)
        o = (_dot(p_p.astype(BF16), vp_ref[:, sl]) + _dot(p_c.astype(BF16), vc_ref[:, sl])) / den
        o_ref[:, sl] = o.astype(o_ref.dtype)
        lse_tile = jnp.where(lane == hd, m + jnp.log(den), lse_tile)
    lse_ref[...] = lse_tile


def _dilated_group_attention(qkv, n_keys):
    bsz, dil, sub, _ = qkv.shape
    nb = sub // ATT_BLOCK
    w = DIL_WIDTH
    cur = lambda part: pl.BlockSpec((None, None, ATT_BLOCK, w), lambda b, c, n: (b, c, n, part))
    prev = lambda part: pl.BlockSpec((None, None, ATT_BLOCK, w),
                                     lambda b, c, n: (b, c, jnp.maximum(n - 1, 0), part))
    o, lse = pl.pallas_call(
        functools.partial(_dil_attn_kernel, max_dist=n_keys),
        grid=(bsz, dil, nb),
        in_specs=[cur(0), cur(1), cur(2), prev(1), prev(2)],
        out_specs=[pl.BlockSpec((None, ATT_BLOCK, w), lambda b, c, n: (b, n, c)),
                   pl.BlockSpec((None, ATT_BLOCK, LANES), lambda b, c, n: (b, n, c))],
        out_shape=[jax.ShapeDtypeStruct((bsz, sub, dil * w), BF16),
                   jax.ShapeDtypeStruct((bsz, sub, dil * LANES), F32)],
        compiler_params=_cparams("parallel", "parallel", "arbitrary"),
        name="dilated_attention",
    )(qkv, qkv, qkv, qkv, qkv)
    return o.reshape(bsz * sub * dil, w), lse.reshape(bsz * sub * dil, LANES)


def _combine_kernel(o0_ref, o1_ref, o2_ref, l0_ref, l1_ref, l2_ref, e_ref, y_ref):
    l0, l1, l2 = l0_ref[...], l1_ref[...], l2_ref[...]
    m = jnp.maximum(jnp.maximum(l0, l1), l2)
    w0, w1, w2 = jnp.exp(l0 - m), jnp.exp(l1 - m), jnp.exp(l2 - m)
    den = w0 + w1 + w2
    tm = l0.shape[0]
    alpha = jnp.concatenate([w0 / den, w1 / den, w2 / den], axis=0)
    ax = _split_dot(alpha, e_ref[...], 2)
    y = (ax[0:tm] * o0_ref[...].astype(F32) + ax[tm:2 * tm] * o1_ref[...].astype(F32)
         + ax[2 * tm:3 * tm] * o2_ref[...].astype(F32))
    y_ref[...] = y.astype(y_ref.dtype)


def _combine_groups(outs, lses, tm=512):
    t, w = outs[0].shape
    head_of_col = jnp.arange(w) // DIL_HEAD_DIM
    expand = (jnp.arange(LANES)[:, None] == head_of_col[None, :]).astype(BF16)
    o_spec = pl.BlockSpec((tm, w), lambda i: (i, 0))
    l_spec = pl.BlockSpec((tm, LANES), lambda i: (i, 0))
    return pl.pallas_call(
        _combine_kernel,
        grid=(t // tm,),
        in_specs=[o_spec, o_spec, o_spec, l_spec, l_spec, l_spec,
                  pl.BlockSpec((LANES, w), lambda i: (0, 0))],
        out_specs=o_spec,
        out_shape=jax.ShapeDtypeStruct((t, w), BF16),
        compiler_params=_cparams("parallel"),
        name="dilated_combine",
    )(*outs, *lses, expand)


def _swa_kernel(sink_ref, q_ref, kc_ref, vc_ref, kp_ref, vp_ref, o_ref):
    prev_ok, cur_ok = _band_masks(pl.program_id(1) == 0, SWA_WINDOW - 1)
    lane = lax.broadcasted_iota(jnp.int32, (ATT_BLOCK, LANES), 1)
    low = lane < SWA_HEAD_DIM
    scale = SWA_HEAD_DIM ** -0.5
    q_per_kv = SWA_Q_HEADS // SWA_KV_HEADS
    for g in range(SWA_KV_HEADS):
        gl = slice(g * LANES, (g + 1) * LANES)
        zero = jnp.zeros((ATT_BLOCK, LANES), BF16)
        halves = []
        for keep in (low, ~low):
            halves.append((jnp.where(keep, kp_ref[:, gl], zero), jnp.where(keep, kc_ref[:, gl], zero),
                           jnp.where(keep, vp_ref[:, gl], zero), jnp.where(keep, vc_ref[:, gl], zero)))
        for pair in range(q_per_kv // 2):
            ql = slice((g * q_per_kv // 2 + pair) * LANES, (g * q_per_kv // 2 + pair + 1) * LANES)
            q = q_ref[:, ql]
            acc = None
            for par in range(2):
                k_p, k_c, v_p, v_c = halves[par]
                sink = sink_ref[g * q_per_kv + 2 * pair + par]
                s_p = jnp.where(prev_ok, _dot_nt(q, k_p) * scale, -jnp.inf)
                s_c = jnp.where(cur_ok, _dot_nt(q, k_c) * scale, -jnp.inf)
                m = jnp.maximum(jnp.maximum(jnp.max(s_p, axis=-1, keepdims=True),
                                            jnp.max(s_c, axis=-1, keepdims=True)), sink)
                p_p = jnp.exp(s_p - m)
                p_c = jnp.exp(s_c - m)
                den = (jnp.sum(p_p, axis=-1, keepdims=True) + jnp.sum(p_c, axis=-1, keepdims=True)
                       + jnp.exp(sink - m))
                term = (_dot(p_p.astype(BF16), v_p) + _dot(p_c.astype(BF16), v_c)) / den
                acc = term if acc is None else acc + term
            o_ref[:, ql] = acc.astype(o_ref.dtype)


def _swa_attention(qkv, sinks, bsz, seq):
    t = qkv.shape[0]
    nb = seq // ATT_BLOCK
    kvw = 2 * ODD_KV
    q_col = ODD_Q // kvw
    return pl.pallas_call(
        _swa_kernel,
        grid=(bsz, nb),
        in_specs=[pl.BlockSpec(memory_space=pltpu.SMEM),
                  pl.BlockSpec((ATT_BLOCK, ODD_Q), lambda b, n: (b * nb + n, 0)),
                  pl.BlockSpec((ATT_BLOCK, kvw), lambda b, n: (b * nb + n, q_col)),
                  pl.BlockSpec((ATT_BLOCK, kvw), lambda b, n: (b * nb + n, q_col + 1)),
                  pl.BlockSpec((ATT_BLOCK, kvw), lambda b, n: (jnp.maximum(b * nb + n - 1, 0), q_col)),
                  pl.BlockSpec((ATT_BLOCK, kvw), lambda b, n: (jnp.maximum(b * nb + n - 1, 0), q_col + 1))],
        out_specs=pl.BlockSpec((ATT_BLOCK, ODD_Q), lambda b, n: (b * nb + n, 0)),
        out_shape=jax.ShapeDtypeStruct((t, ODD_Q), BF16),
        compiler_params=_cparams("parallel", "arbitrary"),
        name="swa_sink_attention",
    )(sinks, qkv, qkv, qkv, qkv, qkv)


def _rope_table_kernel(pos_ref, freq_ref, sign_ref, cos_ref, sin_ref):
    ang = pos_ref[...] * freq_ref[...]
    cos_ref[...] = jnp.cos(ang)
    sin_ref[...] = jnp.sin(ang) * sign_ref[...]


def _rope_tables(positions, tm=1024):
    t = positions.size
    half = SWA_HEAD_DIM // 2
    inv_freq = ROPE_THETA ** (-jnp.arange(half, dtype=F32) / half)
    lane = jnp.arange(LANES)
    freq = inv_freq[lane % half].reshape(1, LANES)
    sign = jnp.where((lane % SWA_HEAD_DIM) < half, -1.0, 1.0).astype(F32).reshape(1, LANES)
    pos = positions.astype(F32).reshape(t, 1)
    row = pl.BlockSpec((tm, LANES), lambda i: (i, 0))
    const = pl.BlockSpec((1, LANES), lambda i: (0, 0))
    return pl.pallas_call(
        _rope_table_kernel,
        grid=(t // tm,),
        in_specs=[pl.BlockSpec((tm, 1), lambda i: (i, 0)), const, const],
        out_specs=[row, row],
        out_shape=[jax.ShapeDtypeStruct((t, LANES), F32)] * 2,
        compiler_params=_cparams("parallel"),
        name="rope_tables",
    )(pos, freq, sign)


def _even_mixer(x, h, bsz, seq, w_in, conv_w, conv_b, dt_bias, a_log, d_skip, gate_norm, w_out):
    zx_end = SSD_D_INNER + SSD_XBC
    dt_end = zx_end + SSD_HEADS
    w_zx = w_in[:, :zx_end]
    w_dt = jnp.pad(w_in[:, zx_end:dt_end], ((0, 0), (0, LANES - SSD_HEADS)))
    zx, dt_raw = _matmul_with_dt(h, w_zx, w_dt)
    y_a = _ssd_branch(zx, dt_raw, conv_w, conv_b, dt_bias, a_log, d_skip, gate_norm, bsz, seq)

    n_groups = len(DIL_PATTERNS)
    outs, lses = [], []
    for g, (window, dilation) in enumerate(DIL_PATTERNS):
        cols = [w_in[:, dt_end + (part * n_groups + g) * DIL_WIDTH:dt_end + (part * n_groups + g + 1) * DIL_WIDTH]
                for part in range(3)]
        qkv = _matmul_dilated_rows(h, jnp.concatenate(cols, axis=1), bsz, seq, dilation)
        o, lse = _dilated_group_attention(qkv, window // dilation)
        outs.append(o)
        lses.append(lse)
    y_b = _combine_groups(outs, lses)
    return _matmul_residual([(y_a, w_out[:SSD_D_INNER]), (y_b, w_out[SSD_D_INNER:])], x)


def _odd_mixer(x, h, bsz, seq, cos, sin, w_in, b_in, sinks, w_out):
    d = w_in.shape[0]

    def doubled(cols):
        c = cols.reshape(cols.shape[:-1] + (SWA_KV_HEADS, SWA_HEAD_DIM))
        return jnp.concatenate([c, c], axis=-1).reshape(cols.shape[:-1] + (2 * ODD_KV,))

    w = jnp.concatenate([w_in[:, :ODD_Q], doubled(w_in[:, ODD_Q:ODD_Q + ODD_KV]),
                         doubled(w_in[:, ODD_Q + ODD_KV:])], axis=1)
    b = jnp.concatenate([b_in[:ODD_Q], doubled(b_in[ODD_Q:ODD_Q + ODD_KV]), doubled(b_in[ODD_Q + ODD_KV:])])
    qkv = _matmul_bias_rope(h, w, b, cos, sin, rope_cols=ODD_Q + 2 * ODD_KV)
    o = _swa_attention(qkv, sinks, bsz, seq)
    return _matmul_residual([(o, w_out)], x)


def kernel(x, mem, positions, ffn1_norm, ffn1_w1, ffn1_w2, mix_norm, even_w_in, even_conv_w, even_conv_b,
           even_dt_bias, even_a_log, even_d_skip, even_gate_norm, even_w_out, odd_w_in, odd_b_in, odd_sinks,
           odd_w_out, mem_norm, mem_w_kv, xa_norm, xa_w_q, xa_w_o, ffn2_norm, ffn2_w1, ffn2_w2, final_norm):
    bsz, seq, d = x.shape
    depth = ffn1_norm.shape[0]
    bf = lambda a: a.astype(BF16)
    x = x.reshape(bsz * seq, d)

    mem_h = _rmsnorm(mem.reshape(bsz * MEM_LEN, d), mem_norm, BF16)
    mem_kv = _matmul(mem_h, bf(mem_w_kv), BF16, name="memory_kv_proj")
    cos, sin = _rope_tables(positions)

    for i in range(depth):
        j = i // 2
        x = _ffn(x, ffn1_norm[i], bf(ffn1_w1[i]), bf(ffn1_w2[i]))
        h = _rmsnorm(x, mix_norm[i], BF16)
        if i % 2 == 0:
            x = _even_mixer(x, h, bsz, seq, bf(even_w_in[j]), even_conv_w[j], even_conv_b[j], even_dt_bias[j],
                            even_a_log[j], even_d_skip[j], even_gate_norm[j], bf(even_w_out[j]))
        else:
            x = _odd_mixer(x, h, bsz, seq, cos, sin, bf(odd_w_in[j]), odd_b_in[j], odd_sinks[j], bf(odd_w_out[j]))
        x = _cross_attention(x, xa_norm[i], bf(xa_w_q[i]), mem_kv, bf(xa_w_o[i]), seq)
        x = _ffn(x, ffn2_norm[i], bf(ffn2_w1[i]), bf(ffn2_w2[i]),
                 final_gain=final_norm if i == depth - 1 else None)
    return x.reshape(bsz, seq, d)
```

```python
import functools
import math

import jax
import jax.numpy as jnp
from jax import lax
from jax.experimental import pallas as pl
from jax.experimental.pallas import tpu as pltpu

F32 = jnp.float32
BF16 = jnp.bfloat16

EPS = 1e-5
MEM_LEN = 256

SSD_HEADS = 32
SSD_HEAD_DIM = 64
SSD_D_INNER = SSD_HEADS * SSD_HEAD_DIM
SSD_GROUPS = 4
SSD_STATE = 128
SSD_CONV = 4
SSD_CHUNK = 128
SSD_XBC = SSD_D_INNER + 2 * SSD_GROUPS * SSD_STATE
SSD_GROUP_WIDTH = SSD_D_INNER // SSD_GROUPS

DIL_PATTERNS = ((128, 1), (512, 4), (2048, 16))
DIL_HEADS = 8
DIL_HEAD_DIM = 128
DIL_WIDTH = DIL_HEADS * DIL_HEAD_DIM

SWA_Q_HEADS = 32
SWA_KV_HEADS = 4
SWA_HEAD_DIM = 64
SWA_WINDOW = 128
ROPE_THETA = 150000.0
ODD_Q = SWA_Q_HEADS * SWA_HEAD_DIM
ODD_KV = SWA_KV_HEADS * SWA_HEAD_DIM

XA_HEADS = 4
XA_HEAD_DIM = 128
XA_WIDTH = XA_HEADS * XA_HEAD_DIM

LANES = 128
ATT_BLOCK = 128
VMEM_LIMIT = 56 * 1024 * 1024


def _cparams(*sem):
    return pltpu.CompilerParams(dimension_semantics=sem, vmem_limit_bytes=VMEM_LIMIT)


def _rms(x, gain):
    ms = jnp.mean(x * x, axis=-1, keepdims=True)
    return x * lax.rsqrt(ms + EPS) * gain


def _sigmoid(x):
    return 1.0 / (1.0 + jnp.exp(-x))


def _dot(a, b):
    return jnp.dot(a, b, preferred_element_type=F32)


def _dot_nt(a, b):
    return lax.dot_general(a, b, (((1,), (1,)), ((), ())), preferred_element_type=F32)


def _split_dot(v, e, parts):
    out = None
    rem = v
    for _ in range(parts):
        piece = rem.astype(BF16)
        term = _dot(piece, e)
        out = term if out is None else out + term
        rem = rem - piece.astype(F32)
    return out


def _rmsnorm_kernel(x_ref, g_ref, o_ref):
    o_ref[...] = _rms(x_ref[...], g_ref[...]).astype(o_ref.dtype)


def _rmsnorm(x, gain, out_dtype, tm=512):
    t, d = x.shape
    tm = min(tm, t)
    return pl.pallas_call(
        _rmsnorm_kernel,
        grid=(t // tm,),
        in_specs=[pl.BlockSpec((tm, d), lambda i: (i, 0)),
                  pl.BlockSpec((1, d), lambda i: (0, 0))],
        out_specs=pl.BlockSpec((tm, d), lambda i: (i, 0)),
        out_shape=jax.ShapeDtypeStruct((t, d), out_dtype),
        compiler_params=_cparams("parallel"),
        name="rmsnorm",
    )(x, gain.reshape(1, d))


def _group_permutation(tm, dil):
    r = jnp.arange(tm)
    grouped = (r % dil) * (tm // dil) + r // dil
    return (jnp.arange(tm)[:, None] == grouped[None, :]).astype(BF16)


def _rmsnorm_grouped_kernel(x_ref, g_ref, *refs, dils):
    n = len(dils)
    perm_refs, h_ref, grouped_refs = refs[:n], refs[n], refs[n + 1:]
    h = _rms(x_ref[...], g_ref[...]).astype(BF16)
    h_ref[...] = h
    tm = h.shape[0]
    for p_ref, o_ref, dil in zip(perm_refs, grouped_refs, dils):
        hp = _dot(p_ref[...], h).astype(BF16)
        rows = tm // dil
        for c in range(dil):
            o_ref[c] = hp[c * rows:(c + 1) * rows]


def _rmsnorm_grouped(x, gain, bsz, seq, dils, tm=512):
    t, d = x.shape
    tiles = seq // tm
    perms = [_group_permutation(tm, dil) for dil in dils]
    outs = pl.pallas_call(
        functools.partial(_rmsnorm_grouped_kernel, dils=dils),
        grid=(bsz, tiles),
        in_specs=[pl.BlockSpec((tm, d), lambda b, i: (b * tiles + i, 0)),
                  pl.BlockSpec((1, d), lambda b, i: (0, 0))]
                 + [pl.BlockSpec((tm, tm), lambda b, i: (0, 0)) for _ in dils],
        out_specs=[pl.BlockSpec((tm, d), lambda b, i: (b * tiles + i, 0))]
                  + [pl.BlockSpec((None, dil, tm // dil, d), lambda b, i: (b, 0, i, 0)) for dil in dils],
        out_shape=[jax.ShapeDtypeStruct((t, d), BF16)]
                  + [jax.ShapeDtypeStruct((bsz, dil, seq // dil, d), BF16) for dil in dils],
        compiler_params=_cparams("parallel", "parallel"),
        name="rmsnorm_grouped",
    )(x, gain.reshape(1, d), *perms)
    return outs[0], [o.reshape(t, d) for o in outs[1:]]


def _mm_kernel(a_ref, w_ref, o_ref):
    o_ref[...] = _dot(a_ref[...], w_ref[...]).astype(o_ref.dtype)


def _matmul(a, w, out_dtype, tm=1024, tn=512, n_cols=None, col_block=None, name="matmul"):
    t, k = a.shape
    n = w.shape[1] if n_cols is None else n_cols
    tm, tn = min(tm, t), min(tn, n)
    col_block = col_block or (lambda j: j)
    return pl.pallas_call(
        _mm_kernel,
        grid=(t // tm, n // tn),
        in_specs=[pl.BlockSpec((tm, k), lambda i, j: (i, 0)),
                  pl.BlockSpec((k, tn), lambda i, j: (0, col_block(j)))],
        out_specs=pl.BlockSpec((tm, tn), lambda i, j: (i, j)),
        out_shape=jax.ShapeDtypeStruct((t, n), out_dtype),
        compiler_params=_cparams("parallel", "arbitrary"),
        name=name,
    )(a, w)


def _mm_dt_kernel(a_ref, w_ref, wdt_ref, o_ref, dt_ref):
    a = a_ref[...]
    o_ref[...] = _dot(a, w_ref[...]).astype(o_ref.dtype)

    @pl.when(pl.program_id(1) == 0)
    def _():
        dt_ref[...] = _dot(a, wdt_ref[...])


def _matmul_with_dt(a, w, n_cols, tm=1024, tn=512):
    t, k = a.shape
    return pl.pallas_call(
        _mm_dt_kernel,
        grid=(t // tm, n_cols // tn),
        in_specs=[pl.BlockSpec((tm, k), lambda i, j: (i, 0)),
                  pl.BlockSpec((k, tn), lambda i, j: (0, j)),
                  pl.BlockSpec((k, LANES), lambda i, j: (0, n_cols // LANES))],
        out_specs=[pl.BlockSpec((tm, tn), lambda i, j: (i, j)),
                   pl.BlockSpec((tm, LANES), lambda i, j: (i, 0))],
        out_shape=[jax.ShapeDtypeStruct((t, n_cols), BF16),
                   jax.ShapeDtypeStruct((t, LANES), F32)],
        compiler_params=_cparams("parallel", "arbitrary"),
        name="ssd_in_proj",
    )(a, w, w)


def _mm_rope_kernel(a_ref, w_ref, b_ref, cos_ref, sin_ref, o_ref, *, rope_tiles):
    j = pl.program_id(1)
    acc = _dot(a_ref[...], w_ref[...]) + b_ref[...]
    tm, tn = acc.shape

    @pl.when(j < rope_tiles)
    def _():
        cos = cos_ref[...]
        sin = sin_ref[...]
        lane = lax.broadcasted_iota(jnp.int32, (tm, LANES), 1)
        first_half = (lane % SWA_HEAD_DIM) < (SWA_HEAD_DIM // 2)
        for ch in range(tn // LANES):
            t = acc[:, ch * LANES:(ch + 1) * LANES]
            partner = jnp.where(first_half,
                                pltpu.roll(t, LANES - SWA_HEAD_DIM // 2, axis=1),
                                pltpu.roll(t, SWA_HEAD_DIM // 2, axis=1))
            o_ref[:, ch * LANES:(ch + 1) * LANES] = (t * cos + partner * sin).astype(o_ref.dtype)

    @pl.when(j >= rope_tiles)
    def _():
        o_ref[...] = acc.astype(o_ref.dtype)


def _matmul_bias_rope(a, w, bias, cos, sin, rope_cols, tm=1024, tn=512):
    t, k = a.shape
    n = w.shape[1]
    return pl.pallas_call(
        functools.partial(_mm_rope_kernel, rope_tiles=rope_cols // tn),
        grid=(t // tm, n // tn),
        in_specs=[pl.BlockSpec((tm, k), lambda i, j: (i, 0)),
                  pl.BlockSpec((k, tn), lambda i, j: (0, j)),
                  pl.BlockSpec((1, tn), lambda i, j: (0, j)),
                  pl.BlockSpec((tm, LANES), lambda i, j: (i, 0)),
                  pl.BlockSpec((tm, LANES), lambda i, j: (i, 0))],
        out_specs=pl.BlockSpec((tm, tn), lambda i, j: (i, j)),
        out_shape=jax.ShapeDtypeStruct((t, n), BF16),
        compiler_params=_cparams("parallel", "arbitrary"),
        name="swa_in_proj",
    )(a, w, bias.reshape(1, n), cos, sin)


def _mm_res_kernel(a_ref, w_ref, r_ref, o_ref):
    o_ref[...] = r_ref[...] + _dot(a_ref[...], w_ref[...])


def _mm2_res_kernel(a1_ref, a2_ref, w1_ref, w2_ref, r_ref, o_ref):
    o_ref[...] = r_ref[...] + _dot(a1_ref[...], w1_ref[...]) + _dot(a2_ref[...], w2_ref[...])


def _matmul_residual(parts, res, tm=1024, tn=512):
    t, n = res.shape
    a_specs = [pl.BlockSpec((tm, a.shape[1]), lambda i, j: (i, 0)) for a, _ in parts]
    w_specs = [pl.BlockSpec((w.shape[0], tn), lambda i, j: (0, j)) for _, w in parts]
    body = _mm_res_kernel if len(parts) == 1 else _mm2_res_kernel
    return pl.pallas_call(
        body,
        grid=(t // tm, n // tn),
        in_specs=a_specs + w_specs + [pl.BlockSpec((tm, tn), lambda i, j: (i, j))],
        out_specs=pl.BlockSpec((tm, tn), lambda i, j: (i, j)),
        out_shape=jax.ShapeDtypeStruct((t, n), F32),
        compiler_params=_cparams("parallel", "arbitrary"),
        name="out_proj_residual",
    )(*[a for a, _ in parts], *[w for _, w in parts], res)


def _ffn_kernel(x_ref, g_ref, w1g_ref, w1u_ref, w2_ref, fg_ref, o_ref, h_scr, *, final_norm):
    k = pl.program_id(1)

    @pl.when(k == 0)
    def _():
        x = x_ref[...]
        h_scr[...] = _rms(x, g_ref[...]).astype(BF16)
        o_ref[...] = x

    h = h_scr[...]
    gate = _dot(h, w1g_ref[...])
    up = _dot(h, w1u_ref[...])
    act = (gate * _sigmoid(gate) * up * 0.5).astype(BF16)
    o_ref[...] += _dot(act, w2_ref[...])

    if final_norm:
        @pl.when(k == pl.num_programs(1) - 1)
        def _():
            o_ref[...] = _rms(o_ref[...], fg_ref[...])


def _ffn(x, gain, w1, w2, final_gain=None, tm=1024, tf=256):
    t, d = x.shape
    dff = w2.shape[0]
    nk = dff // tf
    fg = jnp.ones((d,), F32) if final_gain is None else final_gain
    return pl.pallas_call(
        functools.partial(_ffn_kernel, final_norm=final_gain is not None),
        grid=(t // tm, nk),
        in_specs=[pl.BlockSpec((tm, d), lambda i, k: (i, 0)),
                  pl.BlockSpec((1, d), lambda i, k: (0, 0)),
                  pl.BlockSpec((d, tf), lambda i, k: (0, k)),
                  pl.BlockSpec((d, tf), lambda i, k: (0, nk + k)),
                  pl.BlockSpec((tf, d), lambda i, k: (k, 0)),
                  pl.BlockSpec((1, d), lambda i, k: (0, 0))],
        out_specs=pl.BlockSpec((tm, d), lambda i, k: (i, 0)),
        out_shape=jax.ShapeDtypeStruct((t, d), F32),
        scratch_shapes=[pltpu.VMEM((tm, d), BF16)],
        compiler_params=_cparams("parallel", "arbitrary"),
        name="swiglu_ffn",
    )(x, gain.reshape(1, d), w1, w1, w2, fg.reshape(1, d))


def _xa_kernel(x_ref, g_ref, wq_ref, k_ref, v_ref, wo_ref, o_ref):
    x = x_ref[...]
    h = _rms(x, g_ref[...]).astype(BF16)
    q = _dot(h, wq_ref[...]).astype(BF16)
    scale = XA_HEAD_DIM ** -0.5
    outs = []
    for hd in range(XA_HEADS):
        sl = slice(hd * XA_HEAD_DIM, (hd + 1) * XA_HEAD_DIM)
        s = _dot_nt(q[:, sl], k_ref[:, sl]) * scale
        m = jnp.max(s, axis=-1, keepdims=True)
        p = jnp.exp(s - m)
        den = jnp.sum(p, axis=-1, keepdims=True)
        outs.append((_dot(p.astype(BF16), v_ref[:, sl]) / den).astype(BF16))
    o = jnp.concatenate(outs, axis=-1)
    o_ref[...] = x + _dot(o, wo_ref[...])


def _cross_attention(x, gain, w_q, mem_kv, w_o, seq, tm=512):
    t, d = x.shape
    per_batch = seq // tm
    return pl.pallas_call(
        _xa_kernel,
        grid=(t // tm,),
        in_specs=[pl.BlockSpec((tm, d), lambda i: (i, 0)),
                  pl.BlockSpec((1, d), lambda i: (0, 0)),
                  pl.BlockSpec((d, XA_WIDTH), lambda i: (0, 0)),
                  pl.BlockSpec((MEM_LEN, XA_WIDTH), lambda i: (i // per_batch, 0)),
                  pl.BlockSpec((MEM_LEN, XA_WIDTH), lambda i: (i // per_batch, 1)),
                  pl.BlockSpec((XA_WIDTH, d), lambda i: (0, 0))],
        out_specs=pl.BlockSpec((tm, d), lambda i: (i, 0)),
        out_shape=jax.ShapeDtypeStruct((t, d), F32),
        compiler_params=_cparams("parallel"),
        name="memory_cross_attention",
    )(x, gain.reshape(1, d), w_q, mem_kv, mem_kv, w_o)


def _ssd_kernel(z_ref, xs_ref, bc_ref, dt_ref, cw_ref, cb_ref, dtb_ref, alog_ref, dskip_ref,
                gn_ref, e_ref, o_ref, ext_scr, h_scr):
    L = SSD_CHUNK
    halo = 8
    c = pl.program_id(1)

    @pl.when(c == 0)
    def _():
        ext_scr[0:halo, :] = jnp.zeros((halo, SSD_XBC), F32)
        h_scr[...] = jnp.zeros(h_scr.shape, F32)

    ext_scr[halo:halo + L, 0:SSD_D_INNER] = xs_ref[...].astype(F32)
    ext_scr[halo:halo + L, SSD_D_INNER:SSD_XBC] = bc_ref[...].astype(F32)
    conv = cb_ref[...] + ext_scr[halo - 3:halo - 3 + L, :] * cw_ref[0:1, :]
    for tap in range(1, SSD_CONV):
        conv = conv + ext_scr[halo - 3 + tap:halo - 3 + tap + L, :] * cw_ref[tap:tap + 1, :]
    ext_scr[0:halo, :] = ext_scr[L:L + halo, :]
    xbc = conv * _sigmoid(conv)

    row = lax.broadcasted_iota(jnp.int32, (L, LANES), 0)
    col = lax.broadcasted_iota(jnp.int32, (L, LANES), 1)
    v = dt_ref[...] + dtb_ref[...]
    dt = jnp.maximum(v, 0.0) + jnp.log1p(jnp.exp(-jnp.abs(v)))
    da = jnp.where(col < SSD_HEADS, dt * (-jnp.exp(alog_ref[...])), 0.0)
    cs = da
    shift = 1
    while shift < L:
        cs = cs + jnp.where(row >= shift, pltpu.roll(cs, shift, axis=0), 0.0)
        shift *= 2
    exp_cs = jnp.exp(cs)
    decay_to_end = jnp.exp(cs[L - 1:L, :] - cs)
    cs_t = cs.T

    expanded = _split_dot(jnp.concatenate([dt, exp_cs, decay_to_end], axis=0), e_ref[...], 2)
    dt_x = expanded[0:L]
    exp_cs_x = expanded[L:2 * L]
    dte_x = expanded[2 * L:3 * L]

    causal = row >= col
    low_half = col < SSD_HEAD_DIM
    gw = SSD_GROUP_WIDTH
    for g in range(SSD_GROUPS):
        gs = slice(g * gw, (g + 1) * gw)
        b_f32 = xbc[:, SSD_D_INNER + g * SSD_STATE:SSD_D_INNER + (g + 1) * SSD_STATE]
        b_g = b_f32.astype(BF16)
        b_t = b_f32.T.astype(BF16)
        c_off = SSD_D_INNER + SSD_GROUPS * SSD_STATE
        c_g = xbc[:, c_off + g * SSD_STATE:c_off + (g + 1) * SSD_STATE].astype(BF16)
        x_g = xbc[:, gs]
        xdt = x_g * dt_x[:, gs]
        xdt_bf = xdt.astype(BF16)
        cb = _dot_nt(c_g, b_g)

        h_in = h_scr[g]
        y_off = _dot(c_g, h_in.astype(BF16)) * exp_cs_x[:, gs]
        new_state = _dot(b_t, (xdt * dte_x[:, gs]).astype(BF16))
        h_scr[g] = h_in * exp_cs_x[L - 1:L, gs] + new_state

        y_parts = []
        heads_per_group = SSD_HEADS // SSD_GROUPS
        for pair in range(heads_per_group // 2):
            x_pair = xdt_bf[:, pair * LANES:(pair + 1) * LANES]
            acc = None
            for par in range(2):
                hd = g * heads_per_group + 2 * pair + par
                seg = cs[:, hd:hd + 1] - cs_t[hd:hd + 1, :]
                decay = jnp.exp(jnp.where(causal, seg, -jnp.inf))
                m = (cb * decay).astype(BF16)
                rhs = jnp.where(low_half if par == 0 else ~low_half, x_pair, jnp.zeros_like(x_pair))
                term = _dot(m, rhs)
                acc = term if acc is None else acc + term
            y_parts.append(acc)
        y = jnp.concatenate(y_parts, axis=-1) + y_off + x_g * dskip_ref[:, gs]
        zg = z_ref[:, gs].astype(F32)
        y = y * (zg * _sigmoid(zg))
        y = y * lax.rsqrt(jnp.mean(y * y, axis=-1, keepdims=True) + EPS) * gn_ref[:, gs]
        o_ref[:, gs] = y.astype(o_ref.dtype)


def _ssd_branch(zx, dt_raw, conv_w, conv_b, dt_bias, a_log, d_skip, gate_norm, bsz, seq):
    t = zx.shape[0]
    L = SSD_CHUNK
    nc = seq // L
    pad = LANES - SSD_HEADS
    head_of_col = jnp.arange(SSD_D_INNER) // SSD_HEAD_DIM
    expand = (jnp.arange(LANES)[:, None] == head_of_col[None, :]).astype(BF16)
    row_map = lambda b, c: (b * nc + c, 0)
    const = lambda b, c: (0, 0)
    return pl.pallas_call(
        _ssd_kernel,
        grid=(bsz, nc),
        in_specs=[pl.BlockSpec((L, SSD_D_INNER), row_map),
                  pl.BlockSpec((L, SSD_D_INNER), lambda b, c: (b * nc + c, 1)),
                  pl.BlockSpec((L, 2 * SSD_GROUPS * SSD_STATE), lambda b, c: (b * nc + c, 4)),
                  pl.BlockSpec((L, LANES), row_map),
                  pl.BlockSpec((SSD_CONV, SSD_XBC), const),
                  pl.BlockSpec((1, SSD_XBC), const),
                  pl.BlockSpec((1, LANES), const),
                  pl.BlockSpec((1, LANES), const),
                  pl.BlockSpec((1, SSD_D_INNER), const),
                  pl.BlockSpec((1, SSD_D_INNER), const),
                  pl.BlockSpec((LANES, SSD_D_INNER), const)],
        out_specs=pl.BlockSpec((L, SSD_D_INNER), row_map),
        out_shape=jax.ShapeDtypeStruct((t, SSD_D_INNER), BF16),
        scratch_shapes=[pltpu.VMEM((L + 8, SSD_XBC), F32),
                        pltpu.VMEM((SSD_GROUPS, SSD_STATE, SSD_GROUP_WIDTH), F32)],
        compiler_params=_cparams("arbitrary", "arbitrary"),
        name="ssd_scan",
    )(zx, zx, zx, dt_raw, conv_w, conv_b.reshape(1, -1),
      jnp.pad(dt_bias, (0, pad)).reshape(1, LANES), jnp.pad(a_log, (0, pad)).reshape(1, LANES),
      jnp.repeat(d_skip, SSD_HEAD_DIM).reshape(1, -1), gate_norm.reshape(1, -1), expand)


def _band_bias(first_block, max_dist):
    qi = lax.broadcasted_iota(jnp.int32, (ATT_BLOCK, 2 * ATT_BLOCK), 0)
    kj = lax.broadcasted_iota(jnp.int32, (ATT_BLOCK, 2 * ATT_BLOCK), 1)
    lowest = jnp.maximum(qi + (ATT_BLOCK - max_dist), jnp.where(first_block, ATT_BLOCK, 0))
    valid = (kj >= lowest) & (kj <= qi + ATT_BLOCK)
    return jnp.where(valid, 0.0, -jnp.inf).astype(F32)


def _dil_attn_kernel(q_ref, kc_ref, vc_ref, kp_ref, vp_ref, o_ref, lse_ref, *, max_dist):
    bias = _band_bias(pl.program_id(1) == 0, max_dist)
    scale = DIL_HEAD_DIM ** -0.5
    heads = [slice(hd * DIL_HEAD_DIM, (hd + 1) * DIL_HEAD_DIM) for hd in range(DIL_HEADS)]
    scores = []
    for sl in heads:
        keys = jnp.concatenate([kp_ref[:, sl], kc_ref[:, sl]], axis=0)
        scores.append(_dot_nt(q_ref[:, sl], keys) * scale + bias)
    s = jnp.concatenate(scores, axis=0)
    m = jnp.max(s, axis=-1, keepdims=True)
    p = jnp.exp(s - m).astype(BF16)
    ones = jnp.ones((2 * ATT_BLOCK, LANES), BF16)
    lane = lax.broadcasted_iota(jnp.int32, (ATT_BLOCK, LANES), 1)
    lse_tile = jnp.zeros((ATT_BLOCK, LANES), F32)
    for hd, sl in enumerate(heads):
        rows = slice(hd * ATT_BLOCK, (hd + 1) * ATT_BLOCK)
        values = jnp.concatenate([vp_ref[:, sl], vc_ref[:, sl]], axis=0)
        r = _dot(p[rows], jnp.concatenate([values, ones], axis=1))
        den = r[:, LANES:]
        o_ref[:, sl] = (r[:, :LANES] / den).astype(o_ref.dtype)
        lse_tile = jnp.where(lane == hd, m[rows] + jnp.log(den), lse_tile)
    lse_ref[...] = lse_tile


def _dilated_group_attention(qkv, bsz, seq, dil, n_keys):
    t = qkv.shape[0]
    nb = seq // dil // ATT_BLOCK
    w = DIL_WIDTH
    cur = lambda part: pl.BlockSpec((ATT_BLOCK, w), lambda s, n: (s * nb + n, part))
    prev = lambda part: pl.BlockSpec((ATT_BLOCK, w), lambda s, n: (s * nb + jnp.maximum(n - 1, 0), part))
    return pl.pallas_call(
        functools.partial(_dil_attn_kernel, max_dist=n_keys),
        grid=(bsz * dil, nb),
        in_specs=[cur(0), cur(1), cur(2), prev(1), prev(2)],
        out_specs=[pl.BlockSpec((ATT_BLOCK, w), lambda s, n: (s * nb + n, 0)),
                   pl.BlockSpec((ATT_BLOCK, LANES), lambda s, n: (s * nb + n, 0))],
        out_shape=[jax.ShapeDtypeStruct((t, w), BF16),
                   jax.ShapeDtypeStruct((t, LANES), F32)],
        compiler_params=_cparams("parallel", "arbitrary"),
        name="dilated_attention",
    )(qkv, qkv, qkv, qkv, qkv)


def _combine_kernel(*refs, dils):
    n = len(dils)
    o_refs, l_refs, perm_refs = refs[:n], refs[n:2 * n], refs[2 * n:-2]
    e_ref, y_ref = refs[-2], refs[-1]
    tm = y_ref.shape[0]
    outs, lses = [], []
    perm_iter = iter(perm_refs)
    for o_ref, l_ref, dil in zip(o_refs, l_refs, dils):
        if dil == 1:
            outs.append(o_ref[...].astype(F32))
            lses.append(l_ref[...])
        else:
            ungroup = next(perm_iter)[...]
            outs.append(_dot(ungroup, o_ref[...].reshape(tm, -1)))
            lse = l_ref[...].reshape(tm, LANES)
            acc, rem = None, lse
            for _ in range(3):
                piece = rem.astype(BF16)
                term = _dot(ungroup, piece)
                acc = term if acc is None else acc + term
                rem = rem - piece.astype(F32)
            lses.append(acc)
    m = functools.reduce(jnp.maximum, lses)
    ws = [jnp.exp(l - m) for l in lses]
    den = functools.reduce(lambda a, b: a + b, ws)
    alpha = jnp.concatenate([w / den for w in ws], axis=0)
    ax = _split_dot(alpha, e_ref[...], 2)
    y = None
    for g, o in enumerate(outs):
        term = ax[g * tm:(g + 1) * tm] * o
        y = term if y is None else y + term
    y_ref[...] = y.astype(y_ref.dtype)


def _combine_groups(outs, lses, bsz, seq, dils, tm=512):
    t, w = outs[0].shape
    tiles = seq // tm
    head_of_col = jnp.arange(w) // DIL_HEAD_DIM
    expand = (jnp.arange(LANES)[:, None] == head_of_col[None, :]).astype(BF16)

    def spec(dil, width):
        if dil == 1:
            return pl.BlockSpec((tm, width), lambda b, i: (b * tiles + i, 0))
        return pl.BlockSpec((None, dil, tm // dil, width), lambda b, i: (b, 0, i, 0))

    def view(a, dil):
        return a if dil == 1 else a.reshape(bsz, dil, seq // dil, a.shape[-1])

    perms = [_group_permutation(tm, dil).T for dil in dils if dil != 1]
    return pl.pallas_call(
        functools.partial(_combine_kernel, dils=dils),
        grid=(bsz, tiles),
        in_specs=[spec(dil, w) for dil in dils] + [spec(dil, LANES) for dil in dils]
                 + [pl.BlockSpec((tm, tm), lambda b, i: (0, 0)) for _ in perms]
                 + [pl.BlockSpec((LANES, w), lambda b, i: (0, 0))],
        out_specs=pl.BlockSpec((tm, w), lambda b, i: (b * tiles + i, 0)),
        out_shape=jax.ShapeDtypeStruct((t, w), BF16),
        compiler_params=_cparams("parallel", "parallel"),
        name="dilated_combine",
    )(*[view(o, dil) for o, dil in zip(outs, dils)], *[view(l, dil) for l, dil in zip(lses, dils)],
      *perms, expand)


def _swa_kernel(sink_ref, q_ref, kc_ref, vc_ref, kp_ref, vp_ref, o_ref):
    q_per_kv = SWA_Q_HEADS // SWA_KV_HEADS
    chunks = q_per_kv // 2
    rows = chunks * ATT_BLOCK
    bias = _band_bias(pl.program_id(1) == 0, SWA_WINDOW - 1)
    bias = jnp.concatenate([bias] * chunks, axis=0)
    lane = lax.broadcasted_iota(jnp.int32, (2 * ATT_BLOCK, LANES), 1).astype(F32).astype(BF16)
    halves = (lane < SWA_HEAD_DIM, lane >= SWA_HEAD_DIM)
    zero = jnp.zeros((2 * ATT_BLOCK, LANES), BF16)
    one = jnp.ones((2 * ATT_BLOCK, LANES), BF16)
    scale = SWA_HEAD_DIM ** -0.5

    scores, sinks = [], []
    for g in range(SWA_KV_HEADS):
        gl = slice(g * LANES, (g + 1) * LANES)
        keys = jnp.concatenate([kp_ref[:, gl], kc_ref[:, gl]], axis=0) * scale
        q = jnp.concatenate([q_ref[:, (g * chunks + c) * LANES:(g * chunks + c + 1) * LANES]
                             for c in range(chunks)], axis=0)
        for par in range(2):
            scores.append(_dot_nt(q, jnp.where(halves[par], keys, zero)) + bias)
            sinks.append(jnp.concatenate(
                [jnp.full((ATT_BLOCK, 1), sink_ref[g * q_per_kv + 2 * c + par], F32) for c in range(chunks)],
                axis=0))
    s = jnp.concatenate(scores, axis=0)
    sink = jnp.concatenate(sinks, axis=0)
    m = jnp.maximum(jnp.max(s, axis=-1, keepdims=True), sink)
    p = jnp.exp(s - m).astype(BF16)
    sink_term = jnp.exp(sink - m)

    low_rows = lax.broadcasted_iota(jnp.int32, (rows, LANES), 1) < SWA_HEAD_DIM
    for g in range(SWA_KV_HEADS):
        gl = slice(g * LANES, (g + 1) * LANES)
        values = jnp.concatenate([vp_ref[:, gl], vc_ref[:, gl]], axis=0)
        r = None
        for par in range(2):
            rhs = jnp.concatenate([jnp.where(halves[par], values, zero),
                                   jnp.where(halves[par], one, zero)], axis=1)
            base = (2 * g + par) * rows
            term = _dot(p[base:base + rows], rhs)
            r = term if r is None else r + term
        base = 2 * g * rows
        den = r[:, LANES:] + jnp.where(low_rows, sink_term[base:base + rows],
                                       sink_term[base + rows:base + 2 * rows])
        o = r[:, :LANES] / den
        for c in range(chunks):
            col = (g * chunks + c) * LANES
            o_ref[:, col:col + LANES] = o[c * ATT_BLOCK:(c + 1) * ATT_BLOCK].astype(o_ref.dtype)


def _swa_attention(qkv, sinks, bsz, seq):
    t = qkv.shape[0]
    nb = seq // ATT_BLOCK
    kvw = 2 * ODD_KV
    q_col = ODD_Q // kvw
    return pl.pallas_call(
        _swa_kernel,
        grid=(bsz, nb),
        in_specs=[pl.BlockSpec(memory_space=pltpu.SMEM),
                  pl.BlockSpec((ATT_BLOCK, ODD_Q), lambda b, n: (b * nb + n, 0)),
                  pl.BlockSpec((ATT_BLOCK, kvw), lambda b, n: (b * nb + n, q_col)),
                  pl.BlockSpec((ATT_BLOCK, kvw), lambda b, n: (b * nb + n, q_col + 1)),
                  pl.BlockSpec((ATT_BLOCK, kvw), lambda b, n: (jnp.maximum(b * nb + n - 1, 0), q_col)),
                  pl.BlockSpec((ATT_BLOCK, kvw), lambda b, n: (jnp.maximum(b * nb + n - 1, 0), q_col + 1))],
        out_specs=pl.BlockSpec((ATT_BLOCK, ODD_Q), lambda b, n: (b * nb + n, 0)),
        out_shape=jax.ShapeDtypeStruct((t, ODD_Q), BF16),
        compiler_params=_cparams("parallel", "arbitrary"),
        name="swa_sink_attention",
    )(sinks, qkv, qkv, qkv, qkv, qkv)


def _rope_table_kernel(pos_ref, freq_ref, sign_ref, cos_ref, sin_ref):
    ang = pos_ref[...] * freq_ref[...]
    cos_ref[...] = jnp.cos(ang)
    sin_ref[...] = jnp.sin(ang) * sign_ref[...]


def _rope_tables(positions, tm=1024):
    t = positions.size
    half = SWA_HEAD_DIM // 2
    inv_freq = ROPE_THETA ** (-jnp.arange(half, dtype=F32) / half)
    lane = jnp.arange(LANES)
    freq = inv_freq[lane % half].reshape(1, LANES)
    sign = jnp.where((lane % SWA_HEAD_DIM) < half, -1.0, 1.0).astype(F32).reshape(1, LANES)
    pos = positions.astype(F32).reshape(t, 1)
    row = pl.BlockSpec((tm, LANES), lambda i: (i, 0))
    const = pl.BlockSpec((1, LANES), lambda i: (0, 0))
    return pl.pallas_call(
        _rope_table_kernel,
        grid=(t // tm,),
        in_specs=[pl.BlockSpec((tm, 1), lambda i: (i, 0)), const, const],
        out_specs=[row, row],
        out_shape=[jax.ShapeDtypeStruct((t, LANES), F32)] * 2,
        compiler_params=_cparams("parallel"),
        name="rope_tables",
    )(pos, freq, sign)


def _even_mixer(x, mix_gain, bsz, seq, w_in, conv_w, conv_b, dt_bias, a_log, d_skip, gate_norm, w_out):
    zx_end = SSD_D_INNER + SSD_XBC
    dt_end = zx_end + SSD_HEADS
    dils = tuple(dilation for _, dilation in DIL_PATTERNS)
    h, grouped = _rmsnorm_grouped(x, mix_gain, bsz, seq, [d for d in dils if d != 1])
    grouped = iter(grouped)
    hs = [h if d == 1 else next(grouped) for d in dils]

    zx, dt_raw = _matmul_with_dt(h, w_in.astype(BF16), zx_end)
    y_a = _ssd_branch(zx, dt_raw, conv_w, conv_b, dt_bias, a_log, d_skip, gate_norm, bsz, seq)

    w_qkv = w_in[:, dt_end:].astype(BF16)
    n_groups = len(DIL_PATTERNS)
    tn = 512
    per_part = DIL_WIDTH // tn
    outs, lses = [], []
    for g, (window, dilation) in enumerate(DIL_PATTERNS):
        col_block = lambda j, g=g: ((j // per_part) * n_groups + g) * per_part + j % per_part
        qkv = _matmul(hs[g], w_qkv, BF16, tn=tn, n_cols=3 * DIL_WIDTH, col_block=col_block,
                      name="dilated_qkv_proj")
        o, lse = _dilated_group_attention(qkv, bsz, seq, dilation, window // dilation)
        outs.append(o)
        lses.append(lse)
    y_b = _combine_groups(outs, lses, bsz, seq, dils)
    return _matmul_residual([(y_a, w_out[:SSD_D_INNER]), (y_b, w_out[SSD_D_INNER:])], x)


def _odd_mixer(x, h, bsz, seq, cos, sin, w_in, b_in, sinks, w_out):
    d = w_in.shape[0]

    def doubled(cols):
        c = cols.reshape(cols.shape[:-1] + (SWA_KV_HEADS, SWA_HEAD_DIM))
        return jnp.concatenate([c, c], axis=-1).reshape(cols.shape[:-1] + (2 * ODD_KV,))

    w = jnp.concatenate([w_in[:, :ODD_Q], doubled(w_in[:, ODD_Q:ODD_Q + ODD_KV]),
                         doubled(w_in[:, ODD_Q + ODD_KV:])], axis=1)
    b = jnp.concatenate([b_in[:ODD_Q], doubled(b_in[ODD_Q:ODD_Q + ODD_KV]), doubled(b_in[ODD_Q + ODD_KV:])])
    qkv = _matmul_bias_rope(h, w, b, cos, sin, rope_cols=ODD_Q + 2 * ODD_KV)
    o = _swa_attention(qkv, sinks, bsz, seq)
    return _matmul_residual([(o, w_out)], x)


def kernel(x, mem, positions, ffn1_norm, ffn1_w1, ffn1_w2, mix_norm, even_w_in, even_conv_w, even_conv_b,
           even_dt_bias, even_a_log, even_d_skip, even_gate_norm, even_w_out, odd_w_in, odd_b_in, odd_sinks,
           odd_w_out, mem_norm, mem_w_kv, xa_norm, xa_w_q, xa_w_o, ffn2_norm, ffn2_w1, ffn2_w2, final_norm):
    bsz, seq, d = x.shape
    depth = ffn1_norm.shape[0]
    bf = lambda a: a.astype(BF16)
    x = x.reshape(bsz * seq, d)

    mem_h = _rmsnorm(mem.reshape(bsz * MEM_LEN, d), mem_norm, BF16)
    mem_kv = _matmul(mem_h, bf(mem_w_kv), BF16, name="memory_kv_proj")
    cos, sin = _rope_tables(positions)

    for i in range(depth):
        j = i // 2
        x = _ffn(x, ffn1_norm[i], bf(ffn1_w1[i]), bf(ffn1_w2[i]))
        if i % 2 == 0:
            x = _even_mixer(x, mix_norm[i], bsz, seq, even_w_in[j], even_conv_w[j], even_conv_b[j], even_dt_bias[j],
                            even_a_log[j], even_d_skip[j], even_gate_norm[j], bf(even_w_out[j]))
        else:
            h = _rmsnorm(x, mix_norm[i], BF16)
            x = _odd_mixer(x, h, bsz, seq, cos, sin, bf(odd_w_in[j]), odd_b_in[j], odd_sinks[j], bf(odd_w_out[j]))
        x = _cross_attention(x, xa_norm[i], bf(xa_w_q[i]), mem_kv, bf(xa_w_o[i]), seq)
        x = _ffn(x, ffn2_norm[i], bf(ffn2_w1[i]), bf(ffn2_w2[i]),
                 final_gain=final_norm if i == depth - 1 else None)
    return x.reshape(bsz, seq, d)
```

```python
import functools
import math

import jax
import jax.numpy as jnp
from jax import lax
from jax.experimental import pallas as pl
from jax.experimental.pallas import tpu as pltpu

F32 = jnp.float32
BF16 = jnp.bfloat16

EPS = 1e-5
MEM_LEN = 256

SSD_HEADS = 32
SSD_HEAD_DIM = 64
SSD_D_INNER = SSD_HEADS * SSD_HEAD_DIM
SSD_GROUPS = 4
SSD_STATE = 128
SSD_CONV = 4
SSD_CHUNK = 128
SSD_XBC = SSD_D_INNER + 2 * SSD_GROUPS * SSD_STATE
SSD_GROUP_WIDTH = SSD_D_INNER // SSD_GROUPS

DIL_PATTERNS = ((128, 1), (512, 4), (2048, 16))
DIL_HEADS = 8
DIL_HEAD_DIM = 128
DIL_WIDTH = DIL_HEADS * DIL_HEAD_DIM

SWA_Q_HEADS = 32
SWA_KV_HEADS = 4
SWA_HEAD_DIM = 64
SWA_WINDOW = 128
ROPE_THETA = 150000.0
ODD_Q = SWA_Q_HEADS * SWA_HEAD_DIM
ODD_KV = SWA_KV_HEADS * SWA_HEAD_DIM

XA_HEADS = 4
XA_HEAD_DIM = 128
XA_WIDTH = XA_HEADS * XA_HEAD_DIM

LANES = 128
ATT_BLOCK = 128
VMEM_LIMIT = 56 * 1024 * 1024


def _cparams(*sem):
    return pltpu.CompilerParams(dimension_semantics=sem, vmem_limit_bytes=VMEM_LIMIT)


def _rms(x, gain):
    ms = jnp.mean(x * x, axis=-1, keepdims=True)
    return x * lax.rsqrt(ms + EPS) * gain


def _sigmoid(x):
    return 1.0 / (1.0 + jnp.exp(-x))


def _dot(a, b):
    return jnp.dot(a, b, preferred_element_type=F32)


def _dot_nt(a, b):
    return lax.dot_general(a, b, (((1,), (1,)), ((), ())), preferred_element_type=F32)


def _split_dot(v, e, parts):
    out = None
    rem = v
    for _ in range(parts):
        piece = rem.astype(BF16)
        term = _dot(piece, e)
        out = term if out is None else out + term
        rem = rem - piece.astype(F32)
    return out


def _rmsnorm_kernel(x_ref, g_ref, o_ref):
    o_ref[...] = _rms(x_ref[...], g_ref[...]).astype(o_ref.dtype)


def _rmsnorm(x, gain, out_dtype, tm=512):
    t, d = x.shape
    tm = min(tm, t)
    return pl.pallas_call(
        _rmsnorm_kernel,
        grid=(t // tm,),
        in_specs=[pl.BlockSpec((tm, d), lambda i: (i, 0)),
                  pl.BlockSpec((1, d), lambda i: (0, 0))],
        out_specs=pl.BlockSpec((tm, d), lambda i: (i, 0)),
        out_shape=jax.ShapeDtypeStruct((t, d), out_dtype),
        compiler_params=_cparams("parallel"),
        name="rmsnorm",
    )(x, gain.reshape(1, d))


def _group_permutation(tm, dil):
    r = jnp.arange(tm)
    grouped = (r % dil) * (tm // dil) + r // dil
    return (jnp.arange(tm)[:, None] == grouped[None, :]).astype(BF16)


def _rmsnorm_grouped_kernel(x_ref, g_ref, *refs, dils):
    n = len(dils)
    perm_refs, h_ref, grouped_refs = refs[:n], refs[n], refs[n + 1:]
    h = _rms(x_ref[...], g_ref[...]).astype(BF16)
    h_ref[...] = h
    tm = h.shape[0]
    for p_ref, o_ref, dil in zip(perm_refs, grouped_refs, dils):
        hp = _dot(p_ref[...], h).astype(BF16)
        rows = tm // dil
        for c in range(dil):
            o_ref[c] = hp[c * rows:(c + 1) * rows]


def _rmsnorm_grouped(x, gain, bsz, seq, dils, tm=512):
    t, d = x.shape
    tiles = seq // tm
    perms = [_group_permutation(tm, dil) for dil in dils]
    outs = pl.pallas_call(
        functools.partial(_rmsnorm_grouped_kernel, dils=dils),
        grid=(bsz, tiles),
        in_specs=[pl.BlockSpec((tm, d), lambda b, i: (b * tiles + i, 0)),
                  pl.BlockSpec((1, d), lambda b, i: (0, 0))]
                 + [pl.BlockSpec((tm, tm), lambda b, i: (0, 0)) for _ in dils],
        out_specs=[pl.BlockSpec((tm, d), lambda b, i: (b * tiles + i, 0))]
                  + [pl.BlockSpec((None, dil, tm // dil, d), lambda b, i: (b, 0, i, 0)) for dil in dils],
        out_shape=[jax.ShapeDtypeStruct((t, d), BF16)]
                  + [jax.ShapeDtypeStruct((bsz, dil, seq // dil, d), BF16) for dil in dils],
        compiler_params=_cparams("parallel", "parallel"),
        name="rmsnorm_grouped",
    )(x, gain.reshape(1, d), *perms)
    return outs[0], [o.reshape(t, d) for o in outs[1:]]


def _mm_kernel(a_ref, w_ref, o_ref):
    o_ref[...] = _dot(a_ref[...], w_ref[...]).astype(o_ref.dtype)


def _matmul(a, w, out_dtype, tm=1024, tn=1024, n_cols=None, col_block=None, name="matmul"):
    t, k = a.shape
    n = w.shape[1] if n_cols is None else n_cols
    tm, tn = min(tm, t), min(tn, n)
    col_block = col_block or (lambda j: j)
    return pl.pallas_call(
        _mm_kernel,
        grid=(t // tm, n // tn),
        in_specs=[pl.BlockSpec((tm, k), lambda i, j: (i, 0)),
                  pl.BlockSpec((k, tn), lambda i, j: (0, col_block(j)))],
        out_specs=pl.BlockSpec((tm, tn), lambda i, j: (i, j)),
        out_shape=jax.ShapeDtypeStruct((t, n), out_dtype),
        compiler_params=_cparams("parallel", "arbitrary"),
        name=name,
    )(a, w)


def _mm_dt_kernel(a_ref, w_ref, wdt_ref, o_ref, dt_ref):
    a = a_ref[...]
    o_ref[...] = _dot(a, w_ref[...]).astype(o_ref.dtype)

    @pl.when(pl.program_id(1) == 0)
    def _():
        dt_ref[...] = _dot(a, wdt_ref[...])


def _matmul_with_dt(a, w, n_cols, tm=1024, tn=1024):
    t, k = a.shape
    return pl.pallas_call(
        _mm_dt_kernel,
        grid=(t // tm, n_cols // tn),
        in_specs=[pl.BlockSpec((tm, k), lambda i, j: (i, 0)),
                  pl.BlockSpec((k, tn), lambda i, j: (0, j)),
                  pl.BlockSpec((k, LANES), lambda i, j: (0, n_cols // LANES))],
        out_specs=[pl.BlockSpec((tm, tn), lambda i, j: (i, j)),
                   pl.BlockSpec((tm, LANES), lambda i, j: (i, 0))],
        out_shape=[jax.ShapeDtypeStruct((t, n_cols), BF16),
                   jax.ShapeDtypeStruct((t, LANES), F32)],
        compiler_params=_cparams("parallel", "arbitrary"),
        name="ssd_in_proj",
    )(a, w, w)


def _mm_rope_kernel(a_ref, w_ref, b_ref, cos_ref, sin_ref, o_ref, *, rope_cols):
    j = pl.program_id(1)
    acc = _dot(a_ref[...], w_ref[...]) + b_ref[...]
    tm, tn = acc.shape
    chunks = tn // LANES

    def store(roped_chunks):
        cos = cos_ref[...]
        sin = sin_ref[...]
        lane = lax.broadcasted_iota(jnp.int32, (tm, LANES), 1)
        first_half = (lane % SWA_HEAD_DIM) < (SWA_HEAD_DIM // 2)
        for ch in range(chunks):
            t = acc[:, ch * LANES:(ch + 1) * LANES]
            if ch < roped_chunks:
                partner = jnp.where(first_half,
                                    pltpu.roll(t, LANES - SWA_HEAD_DIM // 2, axis=1),
                                    pltpu.roll(t, SWA_HEAD_DIM // 2, axis=1))
                t = t * cos + partner * sin
            o_ref[:, ch * LANES:(ch + 1) * LANES] = t.astype(o_ref.dtype)

    full_tiles = rope_cols // tn
    partial = (rope_cols - full_tiles * tn) // LANES
    pl.when(j < full_tiles)(lambda: store(chunks))
    pl.when(j == full_tiles)(lambda: store(partial))
    pl.when(j > full_tiles)(lambda: store(0))


def _matmul_bias_rope(a, w, bias, cos, sin, rope_cols, tm=1024, tn=1024):
    t, k = a.shape
    n = w.shape[1]
    return pl.pallas_call(
        functools.partial(_mm_rope_kernel, rope_cols=rope_cols),
        grid=(t // tm, n // tn),
        in_specs=[pl.BlockSpec((tm, k), lambda i, j: (i, 0)),
                  pl.BlockSpec((k, tn), lambda i, j: (0, j)),
                  pl.BlockSpec((1, tn), lambda i, j: (0, j)),
                  pl.BlockSpec((tm, LANES), lambda i, j: (i, 0)),
                  pl.BlockSpec((tm, LANES), lambda i, j: (i, 0))],
        out_specs=pl.BlockSpec((tm, tn), lambda i, j: (i, j)),
        out_shape=jax.ShapeDtypeStruct((t, n), BF16),
        compiler_params=_cparams("parallel", "arbitrary"),
        name="swa_in_proj",
    )(a, w, bias.reshape(1, n), cos, sin)


def _mm_res_kernel(a_ref, w_ref, r_ref, o_ref):
    o_ref[...] = r_ref[...] + _dot(a_ref[...], w_ref[...])


def _mm2_res_kernel(a1_ref, a2_ref, w1_ref, w2_ref, r_ref, o_ref):
    o_ref[...] = r_ref[...] + _dot(a1_ref[...], w1_ref[...]) + _dot(a2_ref[...], w2_ref[...])


def _matmul_residual(parts, res, tm=1024, tn=1024):
    t, n = res.shape
    a_specs = [pl.BlockSpec((tm, a.shape[1]), lambda i, j: (i, 0)) for a, _ in parts]
    w_specs = [pl.BlockSpec((w.shape[0], tn), lambda i, j: (0, j)) for _, w in parts]
    body = _mm_res_kernel if len(parts) == 1 else _mm2_res_kernel
    return pl.pallas_call(
        body,
        grid=(t // tm, n // tn),
        in_specs=a_specs + w_specs + [pl.BlockSpec((tm, tn), lambda i, j: (i, j))],
        out_specs=pl.BlockSpec((tm, tn), lambda i, j: (i, j)),
        out_shape=jax.ShapeDtypeStruct((t, n), F32),
        compiler_params=_cparams("parallel", "arbitrary"),
        name="out_proj_residual",
    )(*[a for a, _ in parts], *[w for _, w in parts], res)


def _ffn_kernel(x_ref, g_ref, w1g_ref, w1u_ref, w2_ref, fg_ref, o_ref, h_scr, *, final_norm):
    k = pl.program_id(1)

    @pl.when(k == 0)
    def _():
        x = x_ref[...]
        h_scr[...] = _rms(x, g_ref[...]).astype(BF16)
        o_ref[...] = x

    h = h_scr[...]
    gate = _dot(h, w1g_ref[...])
    up = _dot(h, w1u_ref[...])
    act = (gate * _sigmoid(gate) * up * 0.5).astype(BF16)
    o_ref[...] += _dot(act, w2_ref[...])

    if final_norm:
        @pl.when(k == pl.num_programs(1) - 1)
        def _():
            o_ref[...] = _rms(o_ref[...], fg_ref[...])


def _ffn(x, gain, w1, w2, final_gain=None, tm=1024, tf=512):
    t, d = x.shape
    dff = w2.shape[0]
    nk = dff // tf
    fg = jnp.ones((d,), F32) if final_gain is None else final_gain
    return pl.pallas_call(
        functools.partial(_ffn_kernel, final_norm=final_gain is not None),
        grid=(t // tm, nk),
        in_specs=[pl.BlockSpec((tm, d), lambda i, k: (i, 0)),
                  pl.BlockSpec((1, d), lambda i, k: (0, 0)),
                  pl.BlockSpec((d, tf), lambda i, k: (0, k)),
                  pl.BlockSpec((d, tf), lambda i, k: (0, nk + k)),
                  pl.BlockSpec((tf, d), lambda i, k: (k, 0)),
                  pl.BlockSpec((1, d), lambda i, k: (0, 0))],
        out_specs=pl.BlockSpec((tm, d), lambda i, k: (i, 0)),
        out_shape=jax.ShapeDtypeStruct((t, d), F32),
        scratch_shapes=[pltpu.VMEM((tm, d), BF16)],
        compiler_params=_cparams("parallel", "arbitrary"),
        name="swiglu_ffn",
    )(x, gain.reshape(1, d), w1, w1, w2, fg.reshape(1, d))


def _xa_kernel(x_ref, g_ref, wq_ref, k_ref, v_ref, wo_ref, o_ref):
    x = x_ref[...]
    h = _rms(x, g_ref[...]).astype(BF16)
    q = _dot(h, wq_ref[...]).astype(BF16)
    scale = XA_HEAD_DIM ** -0.5
    outs = []
    for hd in range(XA_HEADS):
        sl = slice(hd * XA_HEAD_DIM, (hd + 1) * XA_HEAD_DIM)
        s = _dot_nt(q[:, sl], k_ref[:, sl]) * scale
        m = jnp.max(s, axis=-1, keepdims=True)
        p = jnp.exp(s - m)
        den = jnp.sum(p, axis=-1, keepdims=True)
        outs.append((_dot(p.astype(BF16), v_ref[:, sl]) / den).astype(BF16))
    o = jnp.concatenate(outs, axis=-1)
    o_ref[...] = x + _dot(o, wo_ref[...])


def _cross_attention(x, gain, w_q, mem_kv, w_o, seq, tm=1024):
    t, d = x.shape
    per_batch = seq // tm
    return pl.pallas_call(
        _xa_kernel,
        grid=(t // tm,),
        in_specs=[pl.BlockSpec((tm, d), lambda i: (i, 0)),
                  pl.BlockSpec((1, d), lambda i: (0, 0)),
                  pl.BlockSpec((d, XA_WIDTH), lambda i: (0, 0)),
                  pl.BlockSpec((MEM_LEN, XA_WIDTH), lambda i: (i // per_batch, 0)),
                  pl.BlockSpec((MEM_LEN, XA_WIDTH), lambda i: (i // per_batch, 1)),
                  pl.BlockSpec((XA_WIDTH, d), lambda i: (0, 0))],
        out_specs=pl.BlockSpec((tm, d), lambda i: (i, 0)),
        out_shape=jax.ShapeDtypeStruct((t, d), F32),
        compiler_params=_cparams("parallel"),
        name="memory_cross_attention",
    )(x, gain.reshape(1, d), w_q, mem_kv, mem_kv, w_o)


def _ssd_kernel(z_ref, xs_ref, bc_ref, dt_ref, cw_ref, cb_ref, dtb_ref, alog_ref, dskip_ref,
                gn_ref, e_ref, shift_ref, o_ref, prev_scr, h_scr):
    L = SSD_CHUNK
    c = pl.program_id(1)

    @pl.when(c == 0)
    def _():
        prev_scr[...] = jnp.zeros(prev_scr.shape, BF16)
        h_scr[...] = jnp.zeros(h_scr.shape, F32)

    cur = jnp.concatenate([xs_ref[...], bc_ref[...]], axis=1)
    shifted = _dot(shift_ref[...], jnp.concatenate([prev_scr[...], cur], axis=0))
    prev_scr[...] = cur
    conv = cb_ref[...] + cur.astype(F32) * cw_ref[SSD_CONV - 1:SSD_CONV, :]
    for tap in range(SSD_CONV - 1):
        conv = conv + shifted[tap * L:(tap + 1) * L] * cw_ref[tap:tap + 1, :]
    xbc = conv * _sigmoid(conv)

    row = lax.broadcasted_iota(jnp.int32, (L, LANES), 0)
    col = lax.broadcasted_iota(jnp.int32, (L, LANES), 1)
    v = dt_ref[...] + dtb_ref[...]
    dt = jnp.maximum(v, 0.0) + jnp.log1p(jnp.exp(-jnp.abs(v)))
    da = jnp.where(col < SSD_HEADS, dt * (-jnp.exp(alog_ref[...])), 0.0)
    cs = da
    shift = 1
    while shift < L:
        cs = cs + jnp.where(row >= shift, pltpu.roll(cs, shift, axis=0), 0.0)
        shift *= 2
    exp_cs = jnp.exp(cs)
    decay_to_end = jnp.exp(cs[L - 1:L, :] - cs)
    cs_t = cs.T

    expanded = _split_dot(jnp.concatenate([dt, exp_cs, decay_to_end], axis=0), e_ref[...], 2)
    dt_x = expanded[0:L]
    exp_cs_x = expanded[L:2 * L]
    dte_x = expanded[2 * L:3 * L]

    causal = row >= col
    low_half = col < SSD_HEAD_DIM
    gw = SSD_GROUP_WIDTH
    for g in range(SSD_GROUPS):
        gs = slice(g * gw, (g + 1) * gw)
        b_f32 = xbc[:, SSD_D_INNER + g * SSD_STATE:SSD_D_INNER + (g + 1) * SSD_STATE]
        b_g = b_f32.astype(BF16)
        b_t = b_f32.T.astype(BF16)
        c_off = SSD_D_INNER + SSD_GROUPS * SSD_STATE
        c_g = xbc[:, c_off + g * SSD_STATE:c_off + (g + 1) * SSD_STATE].astype(BF16)
        x_g = xbc[:, gs]
        xdt = x_g * dt_x[:, gs]
        xdt_bf = xdt.astype(BF16)
        cb = _dot_nt(c_g, b_g)

        h_in = h_scr[g]
        y_off = _dot(c_g, h_in.astype(BF16)) * exp_cs_x[:, gs]
        new_state = _dot(b_t, (xdt * dte_x[:, gs]).astype(BF16))
        h_scr[g] = h_in * exp_cs_x[L - 1:L, gs] + new_state

        y_parts = []
        heads_per_group = SSD_HEADS // SSD_GROUPS
        for pair in range(heads_per_group // 2):
            x_pair = xdt_bf[:, pair * LANES:(pair + 1) * LANES]
            acc = None
            for par in range(2):
                hd = g * heads_per_group + 2 * pair + par
                seg = cs[:, hd:hd + 1] - cs_t[hd:hd + 1, :]
                decay = jnp.exp(jnp.where(causal, seg, -jnp.inf))
                m = (cb * decay).astype(BF16)
                rhs = jnp.where(low_half if par == 0 else ~low_half, x_pair, jnp.zeros_like(x_pair))
                term = _dot(m, rhs)
                acc = term if acc is None else acc + term
            y_parts.append(acc)
        y = jnp.concatenate(y_parts, axis=-1) + y_off + x_g * dskip_ref[:, gs]
        zg = z_ref[:, gs].astype(F32)
        y = y * (zg * _sigmoid(zg))
        y = y * lax.rsqrt(jnp.mean(y * y, axis=-1, keepdims=True) + EPS) * gn_ref[:, gs]
        o_ref[:, gs] = y.astype(o_ref.dtype)


def _ssd_branch(zx, dt_raw, conv_w, conv_b, dt_bias, a_log, d_skip, gate_norm, bsz, seq):
    t = zx.shape[0]
    L = SSD_CHUNK
    nc = seq // L
    pad = LANES - SSD_HEADS
    head_of_col = jnp.arange(SSD_D_INNER) // SSD_HEAD_DIM
    expand = (jnp.arange(LANES)[:, None] == head_of_col[None, :]).astype(BF16)
    out_row = jnp.arange((SSD_CONV - 1) * L)
    src = L + out_row % L - (SSD_CONV - 1) + out_row // L
    shift = (src[:, None] == jnp.arange(2 * L)[None, :]).astype(BF16)
    row_map = lambda b, c: (b * nc + c, 0)
    const = lambda b, c: (0, 0)
    return pl.pallas_call(
        _ssd_kernel,
        grid=(bsz, nc),
        in_specs=[pl.BlockSpec((L, SSD_D_INNER), row_map),
                  pl.BlockSpec((L, SSD_D_INNER), lambda b, c: (b * nc + c, 1)),
                  pl.BlockSpec((L, 2 * SSD_GROUPS * SSD_STATE), lambda b, c: (b * nc + c, 4)),
                  pl.BlockSpec((L, LANES), row_map),
                  pl.BlockSpec((SSD_CONV, SSD_XBC), const),
                  pl.BlockSpec((1, SSD_XBC), const),
                  pl.BlockSpec((1, LANES), const),
                  pl.BlockSpec((1, LANES), const),
                  pl.BlockSpec((1, SSD_D_INNER), const),
                  pl.BlockSpec((1, SSD_D_INNER), const),
                  pl.BlockSpec((LANES, SSD_D_INNER), const),
                  pl.BlockSpec(((SSD_CONV - 1) * L, 2 * L), const)],
        out_specs=pl.BlockSpec((L, SSD_D_INNER), row_map),
        out_shape=jax.ShapeDtypeStruct((t, SSD_D_INNER), BF16),
        scratch_shapes=[pltpu.VMEM((L, SSD_XBC), BF16),
                        pltpu.VMEM((SSD_GROUPS, SSD_STATE, SSD_GROUP_WIDTH), F32)],
        compiler_params=_cparams("arbitrary", "arbitrary"),
        name="ssd_scan",
    )(zx, zx, zx, dt_raw, conv_w, conv_b.reshape(1, -1),
      jnp.pad(dt_bias, (0, pad)).reshape(1, LANES), jnp.pad(a_log, (0, pad)).reshape(1, LANES),
      jnp.repeat(d_skip, SSD_HEAD_DIM).reshape(1, -1), gate_norm.reshape(1, -1), expand, shift)


def _band_bias(first_block, max_dist):
    qi = lax.broadcasted_iota(jnp.int32, (ATT_BLOCK, 2 * ATT_BLOCK), 0)
    kj = lax.broadcasted_iota(jnp.int32, (ATT_BLOCK, 2 * ATT_BLOCK), 1)
    lowest = jnp.maximum(qi + (ATT_BLOCK - max_dist), jnp.where(first_block, ATT_BLOCK, 0))
    valid = (kj >= lowest) & (kj <= qi + ATT_BLOCK)
    return jnp.where(valid, 0.0, -jnp.inf).astype(F32)


def _dil_attn_kernel(q_ref, kc_ref, vc_ref, kp_ref, vp_ref, o_ref, lse_ref, *, max_dist):
    scale = DIL_HEAD_DIM ** -0.5
    heads = [slice(hd * DIL_HEAD_DIM, (hd + 1) * DIL_HEAD_DIM) for hd in range(DIL_HEADS)]
    ones = jnp.ones((2 * ATT_BLOCK, LANES), BF16)
    lane = lax.broadcasted_iota(jnp.int32, (ATT_BLOCK, LANES), 1)
    for blk in range(q_ref.shape[0] // ATT_BLOCK):
        cur = slice(blk * ATT_BLOCK, (blk + 1) * ATT_BLOCK)
        prev = slice((blk - 1) * ATT_BLOCK, blk * ATT_BLOCK)
        bias = _band_bias((pl.program_id(1) == 0) if blk == 0 else False, max_dist)
        scores = []
        for sl in heads:
            k_prev = kp_ref[:, sl] if blk == 0 else kc_ref[prev, sl]
            keys = jnp.concatenate([k_prev, kc_ref[cur, sl]], axis=0)
            scores.append(_dot_nt(q_ref[cur, sl], keys) * scale + bias)
        s = jnp.concatenate(scores, axis=0)
        m = jnp.max(s, axis=-1, keepdims=True)
        p = jnp.exp(s - m).astype(BF16)
        lse_tile = jnp.zeros((ATT_BLOCK, LANES), F32)
        for hd, sl in enumerate(heads):
            rows = slice(hd * ATT_BLOCK, (hd + 1) * ATT_BLOCK)
            v_prev = vp_ref[:, sl] if blk == 0 else vc_ref[prev, sl]
            values = jnp.concatenate([v_prev, vc_ref[cur, sl]], axis=0)
            r = _dot(p[rows], jnp.concatenate([values, ones], axis=1))
            den = r[:, LANES:]
            o_ref[cur, sl] = (r[:, :LANES] / den).astype(o_ref.dtype)
            lse_tile = jnp.where(lane == hd, m[rows] + jnp.log(den), lse_tile)
        lse_ref[cur, :] = lse_tile


def _dilated_group_attention(qkv, bsz, seq, dil, n_keys, blocks_per_tile=4):
    t = qkv.shape[0]
    bpt = min(blocks_per_tile, seq // dil // ATT_BLOCK)
    tq = bpt * ATT_BLOCK
    nb = seq // dil // tq
    w = DIL_WIDTH
    cur = lambda part: pl.BlockSpec((tq, w), lambda s, n: (s * nb + n, part))
    prev = lambda part: pl.BlockSpec((ATT_BLOCK, w),
                                     lambda s, n: (jnp.maximum((s * nb + n) * bpt - 1, 0), part))
    return pl.pallas_call(
        functools.partial(_dil_attn_kernel, max_dist=n_keys),
        grid=(bsz * dil, nb),
        in_specs=[cur(0), cur(1), cur(2), prev(1), prev(2)],
        out_specs=[pl.BlockSpec((tq, w), lambda s, n: (s * nb + n, 0)),
                   pl.BlockSpec((tq, LANES), lambda s, n: (s * nb + n, 0))],
        out_shape=[jax.ShapeDtypeStruct((t, w), BF16),
                   jax.ShapeDtypeStruct((t, LANES), F32)],
        compiler_params=_cparams("parallel", "arbitrary"),
        name="dilated_attention",
    )(qkv, qkv, qkv, qkv, qkv)


def _combine_kernel(*refs, dils):
    n = len(dils)
    o_refs, l_refs, perm_refs = refs[:n], refs[n:2 * n], refs[2 * n:-2]
    e_ref, y_ref = refs[-2], refs[-1]
    tm = y_ref.shape[0]
    outs, lses = [], []
    perm_iter = iter(perm_refs)
    for o_ref, l_ref, dil in zip(o_refs, l_refs, dils):
        if dil == 1:
            outs.append(o_ref[...].astype(F32))
            lses.append(l_ref[...])
        else:
            ungroup = next(perm_iter)[...]
            outs.append(_dot(ungroup, o_ref[...].reshape(tm, -1)))
            lse = l_ref[...].reshape(tm, LANES)
            acc, rem = None, lse
            for _ in range(3):
                piece = rem.astype(BF16)
                term = _dot(ungroup, piece)
                acc = term if acc is None else acc + term
                rem = rem - piece.astype(F32)
            lses.append(acc)
    m = functools.reduce(jnp.maximum, lses)
    ws = [jnp.exp(l - m) for l in lses]
    den = functools.reduce(lambda a, b: a + b, ws)
    alpha = jnp.concatenate([w / den for w in ws], axis=0)
    ax = _split_dot(alpha, e_ref[...], 2)
    y = None
    for g, o in enumerate(outs):
        term = ax[g * tm:(g + 1) * tm] * o
        y = term if y is None else y + term
    y_ref[...] = y.astype(y_ref.dtype)


def _combine_groups(outs, lses, bsz, seq, dils, tm=512):
    t, w = outs[0].shape
    tiles = seq // tm
    head_of_col = jnp.arange(w) // DIL_HEAD_DIM
    expand = (jnp.arange(LANES)[:, None] == head_of_col[None, :]).astype(BF16)

    def spec(dil, width):
        if dil == 1:
            return pl.BlockSpec((tm, width), lambda b, i: (b * tiles + i, 0))
        return pl.BlockSpec((None, dil, tm // dil, width), lambda b, i: (b, 0, i, 0))

    def view(a, dil):
        return a if dil == 1 else a.reshape(bsz, dil, seq // dil, a.shape[-1])

    perms = [_group_permutation(tm, dil).T for dil in dils if dil != 1]
    return pl.pallas_call(
        functools.partial(_combine_kernel, dils=dils),
        grid=(bsz, tiles),
        in_specs=[spec(dil, w) for dil in dils] + [spec(dil, LANES) for dil in dils]
                 + [pl.BlockSpec((tm, tm), lambda b, i: (0, 0)) for _ in perms]
                 + [pl.BlockSpec((LANES, w), lambda b, i: (0, 0))],
        out_specs=pl.BlockSpec((tm, w), lambda b, i: (b * tiles + i, 0)),
        out_shape=jax.ShapeDtypeStruct((t, w), BF16),
        compiler_params=_cparams("parallel", "parallel"),
        name="dilated_combine",
    )(*[view(o, dil) for o, dil in zip(outs, dils)], *[view(l, dil) for l, dil in zip(lses, dils)],
      *perms, expand)


def _swa_kernel(sink_ref, q_ref, kc_ref, vc_ref, kp_ref, vp_ref, o_ref):
    q_per_kv = SWA_Q_HEADS // SWA_KV_HEADS
    chunks = q_per_kv // 2
    rows = chunks * ATT_BLOCK
    lane = lax.broadcasted_iota(jnp.int32, (2 * ATT_BLOCK, LANES), 1).astype(F32).astype(BF16)
    key_row = lax.broadcasted_iota(jnp.int32, (2 * ATT_BLOCK, LANES), 0).astype(F32).astype(BF16)
    halves = (lane < SWA_HEAD_DIM, lane >= SWA_HEAD_DIM)
    zero = jnp.zeros((2 * ATT_BLOCK, LANES), BF16)
    one = jnp.ones((2 * ATT_BLOCK, LANES), BF16)
    sink_slot = lax.broadcasted_iota(jnp.int32, (ATT_BLOCK, 2 * ATT_BLOCK), 1) == 0
    scale = SWA_HEAD_DIM ** -0.5

    for blk in range(q_ref.shape[0] // ATT_BLOCK):
        cur = slice(blk * ATT_BLOCK, (blk + 1) * ATT_BLOCK)
        prev = slice((blk - 1) * ATT_BLOCK, blk * ATT_BLOCK)
        bias = _band_bias((pl.program_id(1) == 0) if blk == 0 else False, SWA_WINDOW - 1)
        scores = []
        for g in range(SWA_KV_HEADS):
            gl = slice(g * LANES, (g + 1) * LANES)
            k_prev = kp_ref[:, gl] if blk == 0 else kc_ref[prev, gl]
            keys = jnp.concatenate([k_prev, kc_ref[cur, gl]], axis=0) * scale
            q = jnp.concatenate([q_ref[cur, (g * chunks + c) * LANES:(g * chunks + c + 1) * LANES]
                                 for c in range(chunks)], axis=0)
            for par in range(2):
                qk = _dot_nt(q, jnp.where(halves[par], keys, zero))
                for c in range(chunks):
                    sink = sink_ref[g * q_per_kv + 2 * c + par]
                    scores.append(jnp.where(sink_slot, sink, qk[c * ATT_BLOCK:(c + 1) * ATT_BLOCK] + bias))
        s = jnp.concatenate(scores, axis=0)
        m = jnp.max(s, axis=-1, keepdims=True)
        p = jnp.exp(s - m).astype(BF16)

        for g in range(SWA_KV_HEADS):
            gl = slice(g * LANES, (g + 1) * LANES)
            v_prev = vp_ref[:, gl] if blk == 0 else vc_ref[prev, gl]
            values = jnp.concatenate([v_prev, vc_ref[cur, gl]], axis=0)
            values = jnp.where(key_row < 1, zero, values)
            r = None
            for par in range(2):
                rhs = jnp.concatenate([jnp.where(halves[par], values, zero),
                                       jnp.where(halves[par], one, zero)], axis=1)
                base = (2 * g + par) * rows
                term = _dot(p[base:base + rows], rhs)
                r = term if r is None else r + term
            o = r[:, :LANES] / r[:, LANES:]
            for c in range(chunks):
                col = (g * chunks + c) * LANES
                o_ref[cur, col:col + LANES] = o[c * ATT_BLOCK:(c + 1) * ATT_BLOCK].astype(o_ref.dtype)


def _swa_attention(qkv, sinks, bsz, seq, blocks_per_tile=2):
    t = qkv.shape[0]
    tq = blocks_per_tile * ATT_BLOCK
    nb = seq // tq
    kvw = 2 * ODD_KV
    q_col = ODD_Q // kvw
    prev_block = lambda b, n: jnp.maximum((b * nb + n) * blocks_per_tile - 1, 0)
    return pl.pallas_call(
        _swa_kernel,
        grid=(bsz, nb),
        in_specs=[pl.BlockSpec(memory_space=pltpu.SMEM),
                  pl.BlockSpec((tq, ODD_Q), lambda b, n: (b * nb + n, 0)),
                  pl.BlockSpec((tq, kvw), lambda b, n: (b * nb + n, q_col)),
                  pl.BlockSpec((tq, kvw), lambda b, n: (b * nb + n, q_col + 1)),
                  pl.BlockSpec((ATT_BLOCK, kvw), lambda b, n: (prev_block(b, n), q_col)),
                  pl.BlockSpec((ATT_BLOCK, kvw), lambda b, n: (prev_block(b, n), q_col + 1))],
        out_specs=pl.BlockSpec((tq, ODD_Q), lambda b, n: (b * nb + n, 0)),
        out_shape=jax.ShapeDtypeStruct((t, ODD_Q), BF16),
        compiler_params=_cparams("parallel", "arbitrary"),
        name="swa_sink_attention",
    )(sinks, qkv, qkv, qkv, qkv, qkv)


def _rope_table_kernel(pos_ref, freq_ref, sign_ref, cos_ref, sin_ref):
    ang = pos_ref[...] * freq_ref[...]
    cos_ref[...] = jnp.cos(ang)
    sin_ref[...] = jnp.sin(ang) * sign_ref[...]


def _rope_tables(positions, tm=1024):
    t = positions.size
    half = SWA_HEAD_DIM // 2
    inv_freq = ROPE_THETA ** (-jnp.arange(half, dtype=F32) / half)
    lane = jnp.arange(LANES)
    freq = inv_freq[lane % half].reshape(1, LANES)
    sign = jnp.where((lane % SWA_HEAD_DIM) < half, -1.0, 1.0).astype(F32).reshape(1, LANES)
    pos = positions.astype(F32).reshape(t, 1)
    row = pl.BlockSpec((tm, LANES), lambda i: (i, 0))
    const = pl.BlockSpec((1, LANES), lambda i: (0, 0))
    return pl.pallas_call(
        _rope_table_kernel,
        grid=(t // tm,),
        in_specs=[pl.BlockSpec((tm, 1), lambda i: (i, 0)), const, const],
        out_specs=[row, row],
        out_shape=[jax.ShapeDtypeStruct((t, LANES), F32)] * 2,
        compiler_params=_cparams("parallel"),
        name="rope_tables",
    )(pos, freq, sign)


def _even_mixer(x, mix_gain, bsz, seq, w_in, conv_w, conv_b, dt_bias, a_log, d_skip, gate_norm, w_out):
    zx_end = SSD_D_INNER + SSD_XBC
    dt_end = zx_end + SSD_HEADS
    dils = tuple(dilation for _, dilation in DIL_PATTERNS)
    h, grouped = _rmsnorm_grouped(x, mix_gain, bsz, seq, [d for d in dils if d != 1])
    grouped = iter(grouped)
    hs = [h if d == 1 else next(grouped) for d in dils]

    zx, dt_raw = _matmul_with_dt(h, w_in.astype(BF16), zx_end)
    y_a = _ssd_branch(zx, dt_raw, conv_w, conv_b, dt_bias, a_log, d_skip, gate_norm, bsz, seq)

    w_qkv = w_in[:, dt_end:].astype(BF16)
    n_groups = len(DIL_PATTERNS)
    tn = 1024
    per_part = DIL_WIDTH // tn
    outs, lses = [], []
    for g, (window, dilation) in enumerate(DIL_PATTERNS):
        col_block = lambda j, g=g: ((j // per_part) * n_groups + g) * per_part + j % per_part
        qkv = _matmul(hs[g], w_qkv, BF16, tn=tn, n_cols=3 * DIL_WIDTH, col_block=col_block,
                      name="dilated_qkv_proj")
        o, lse = _dilated_group_attention(qkv, bsz, seq, dilation, window // dilation)
        outs.append(o)
        lses.append(lse)
    y_b = _combine_groups(outs, lses, bsz, seq, dils)
    return _matmul_residual([(y_a, w_out[:SSD_D_INNER]), (y_b, w_out[SSD_D_INNER:])], x)


def _odd_mixer(x, h, bsz, seq, cos, sin, w_in, b_in, sinks, w_out):
    d = w_in.shape[0]

    def doubled(cols):
        c = cols.reshape(cols.shape[:-1] + (SWA_KV_HEADS, SWA_HEAD_DIM))
        return jnp.concatenate([c, c], axis=-1).reshape(cols.shape[:-1] + (2 * ODD_KV,))

    w = jnp.concatenate([w_in[:, :ODD_Q], doubled(w_in[:, ODD_Q:ODD_Q + ODD_KV]),
                         doubled(w_in[:, ODD_Q + ODD_KV:])], axis=1)
    b = jnp.concatenate([b_in[:ODD_Q], doubled(b_in[ODD_Q:ODD_Q + ODD_KV]), doubled(b_in[ODD_Q + ODD_KV:])])
    qkv = _matmul_bias_rope(h, w, b, cos, sin, rope_cols=ODD_Q + 2 * ODD_KV)
    o = _swa_attention(qkv, sinks, bsz, seq)
    return _matmul_residual([(o, w_out)], x)


def kernel(x, mem, positions, ffn1_norm, ffn1_w1, ffn1_w2, mix_norm, even_w_in, even_conv_w, even_conv_b,
           even_dt_bias, even_a_log, even_d_skip, even_gate_norm, even_w_out, odd_w_in, odd_b_in, odd_sinks,
           odd_w_out, mem_norm, mem_w_kv, xa_norm, xa_w_q, xa_w_o, ffn2_norm, ffn2_w1, ffn2_w2, final_norm):
    bsz, seq, d = x.shape
    depth = ffn1_norm.shape[0]
    bf = lambda a: a.astype(BF16)
    x = x.reshape(bsz * seq, d)

    mem_h = _rmsnorm(mem.reshape(bsz * MEM_LEN, d), mem_norm, BF16)
    mem_kv = _matmul(mem_h, bf(mem_w_kv), BF16, name="memory_kv_proj")
    cos, sin = _rope_tables(positions)

    for i in range(depth):
        j = i // 2
        x = _ffn(x, ffn1_norm[i], bf(ffn1_w1[i]), bf(ffn1_w2[i]))
        if i % 2 == 0:
            x = _even_mixer(x, mix_norm[i], bsz, seq, even_w_in[j], even_conv_w[j], even_conv_b[j], even_dt_bias[j],
                            even_a_log[j], even_d_skip[j], even_gate_norm[j], bf(even_w_out[j]))
        else:
            h = _rmsnorm(x, mix_norm[i], BF16)
            x = _odd_mixer(x, h, bsz, seq, cos, sin, bf(odd_w_in[j]), odd_b_in[j], odd_sinks[j], bf(odd_w_out[j]))
        x = _cross_attention(x, xa_norm[i], bf(xa_w_q[i]), mem_kv, bf(xa_w_o[i]), seq)
        x = _ffn(x, ffn2_norm[i], bf(ffn2_w1[i]), bf(ffn2_w2[i]),
                 final_gain=final_norm if i == depth - 1 else None)
    return x.reshape(bsz, seq, d)
```

```python
import functools
import math

import jax
import jax.numpy as jnp
import numpy as np
from jax import lax
from jax.experimental import pallas as pl
from jax.experimental.pallas import tpu as pltpu

F32 = jnp.float32
BF16 = jnp.bfloat16

EPS = 1e-5
MEM_LEN = 256

SSD_HEADS = 32
SSD_HEAD_DIM = 64
SSD_D_INNER = SSD_HEADS * SSD_HEAD_DIM
SSD_GROUPS = 4
SSD_STATE = 128
SSD_CONV = 4
SSD_CHUNK = 128
SSD_XBC = SSD_D_INNER + 2 * SSD_GROUPS * SSD_STATE
SSD_GROUP_WIDTH = SSD_D_INNER // SSD_GROUPS

DIL_PATTERNS = ((128, 1), (512, 4), (2048, 16))
DIL_HEADS = 8
DIL_HEAD_DIM = 128
DIL_WIDTH = DIL_HEADS * DIL_HEAD_DIM

SWA_Q_HEADS = 32
SWA_KV_HEADS = 4
SWA_HEAD_DIM = 64
SWA_WINDOW = 128
ROPE_THETA = 150000.0
ODD_Q = SWA_Q_HEADS * SWA_HEAD_DIM
ODD_KV = SWA_KV_HEADS * SWA_HEAD_DIM

XA_HEADS = 4
XA_HEAD_DIM = 128
XA_WIDTH = XA_HEADS * XA_HEAD_DIM

LANES = 128
ATT_BLOCK = 128
VMEM_LIMIT = 56 * 1024 * 1024


def _cparams(*sem):
    return pltpu.CompilerParams(dimension_semantics=sem, vmem_limit_bytes=VMEM_LIMIT)


def _rms(x, gain):
    ms = jnp.mean(x * x, axis=-1, keepdims=True)
    return x * lax.rsqrt(ms + EPS) * gain


def _sigmoid(x):
    return 1.0 / (1.0 + jnp.exp(-x))


def _dot(a, b):
    return jnp.dot(a, b, preferred_element_type=F32)


def _dot_nt(a, b):
    return lax.dot_general(a, b, (((1,), (1,)), ((), ())), preferred_element_type=F32)


def _split_dot(v, e, parts):
    out = None
    rem = v
    for _ in range(parts):
        piece = rem.astype(BF16)
        term = _dot(piece, e)
        out = term if out is None else out + term
        rem = rem - piece.astype(F32)
    return out


def _rmsnorm_kernel(x_ref, g_ref, o_ref):
    o_ref[...] = _rms(x_ref[...], g_ref[...]).astype(o_ref.dtype)


def _rmsnorm(x, gain, out_dtype, tm=512):
    t, d = x.shape
    tm = min(tm, t)
    return pl.pallas_call(
        _rmsnorm_kernel,
        grid=(t // tm,),
        in_specs=[pl.BlockSpec((tm, d), lambda i: (i, 0)),
                  pl.BlockSpec((1, d), lambda i: (0, 0))],
        out_specs=pl.BlockSpec((tm, d), lambda i: (i, 0)),
        out_shape=jax.ShapeDtypeStruct((t, d), out_dtype),
        compiler_params=_cparams("parallel"),
        name="rmsnorm",
    )(x, gain.reshape(1, d))


def _group_permutation(tm, dil, inverse=False):
    r = np.arange(tm)
    grouped = (r % dil) * (tm // dil) + r // dil
    g = np.arange(tm)[:, None] == grouped[None, :]
    return jnp.asarray(g.T if inverse else g, BF16)


def _head_expansion(width, head_dim):
    return jnp.asarray(np.arange(LANES)[:, None] == (np.arange(width) // head_dim)[None, :], BF16)


def _rmsnorm_grouped_kernel(x_ref, g_ref, *refs, dils):
    n = len(dils)
    perm_refs, h_ref, grouped_refs = refs[:n], refs[n], refs[n + 1:]
    h = _rms(x_ref[...], g_ref[...]).astype(BF16)
    h_ref[...] = h
    tm = h.shape[0]
    for p_ref, o_ref, dil in zip(perm_refs, grouped_refs, dils):
        hp = _dot(p_ref[...], h).astype(BF16)
        rows = tm // dil
        for c in range(dil):
            o_ref[c] = hp[c * rows:(c + 1) * rows]


def _rmsnorm_grouped(x, gain, bsz, seq, dils, tm=512):
    t, d = x.shape
    tiles = seq // tm
    perms = [_group_permutation(tm, dil) for dil in dils]
    outs = pl.pallas_call(
        functools.partial(_rmsnorm_grouped_kernel, dils=dils),
        grid=(bsz, tiles),
        in_specs=[pl.BlockSpec((tm, d), lambda b, i: (b * tiles + i, 0)),
                  pl.BlockSpec((1, d), lambda b, i: (0, 0))]
                 + [pl.BlockSpec((tm, tm), lambda b, i: (0, 0)) for _ in dils],
        out_specs=[pl.BlockSpec((tm, d), lambda b, i: (b * tiles + i, 0))]
                  + [pl.BlockSpec((None, dil, tm // dil, d), lambda b, i: (b, 0, i, 0)) for dil in dils],
        out_shape=[jax.ShapeDtypeStruct((t, d), BF16)]
                  + [jax.ShapeDtypeStruct((bsz, dil, seq // dil, d), BF16) for dil in dils],
        compiler_params=_cparams("parallel", "parallel"),
        name="rmsnorm_grouped",
    )(x, gain.reshape(1, d), *perms)
    return outs[0], [o.reshape(t, d) for o in outs[1:]]


def _wspec(block, index, layer=None):
    if layer is None:
        return pl.BlockSpec(block, index)
    return pl.BlockSpec((None,) + block, lambda *g: (layer,) + index(*g))


def _mm_kernel(a_ref, w_ref, o_ref):
    o_ref[...] = _dot(a_ref[...], w_ref[...]).astype(o_ref.dtype)


def _matmul(a, w, out_dtype, tm=1024, tn=1024, n_cols=None, col_block=None, layer=None, name="matmul"):
    t, k = a.shape
    n = w.shape[-1] if n_cols is None else n_cols
    tm, tn = min(tm, t), min(tn, n)
    col_block = col_block or (lambda j: j)
    return pl.pallas_call(
        _mm_kernel,
        grid=(t // tm, n // tn),
        in_specs=[pl.BlockSpec((tm, k), lambda i, j: (i, 0)),
                  _wspec((k, tn), lambda i, j: (0, col_block(j)), layer)],
        out_specs=pl.BlockSpec((tm, tn), lambda i, j: (i, j)),
        out_shape=jax.ShapeDtypeStruct((t, n), out_dtype),
        compiler_params=_cparams("parallel", "arbitrary"),
        name=name,
    )(a, w)


def _mm_dt_kernel(a_ref, w_ref, wdt_ref, o_ref, dt_ref):
    a = a_ref[...]
    o_ref[...] = _dot(a, w_ref[...]).astype(o_ref.dtype)

    @pl.when(pl.program_id(1) == 0)
    def _():
        dt_ref[...] = _dot(a, wdt_ref[...])


def _matmul_with_dt(a, w, layer, n_cols, tm=1024, tn=1024):
    t, k = a.shape
    return pl.pallas_call(
        _mm_dt_kernel,
        grid=(t // tm, n_cols // tn),
        in_specs=[pl.BlockSpec((tm, k), lambda i, j: (i, 0)),
                  _wspec((k, tn), lambda i, j: (0, j), layer),
                  _wspec((k, LANES), lambda i, j: (0, n_cols // LANES), layer)],
        out_specs=[pl.BlockSpec((tm, tn), lambda i, j: (i, j)),
                   pl.BlockSpec((tm, LANES), lambda i, j: (i, 0))],
        out_shape=[jax.ShapeDtypeStruct((t, n_cols), BF16),
                   jax.ShapeDtypeStruct((t, LANES), F32)],
        compiler_params=_cparams("parallel", "arbitrary"),
        name="ssd_in_proj",
    )(a, w, w)


def _mm_rope_kernel(a_ref, w_ref, b_ref, cos_ref, sin_ref, o_ref, *, rope_cols):
    j = pl.program_id(1)
    acc = _dot(a_ref[...], w_ref[...]) + b_ref[...]
    tm, tn = acc.shape
    chunks = tn // LANES

    def store(roped_chunks):
        cos = cos_ref[...]
        sin = sin_ref[...]
        lane = lax.broadcasted_iota(jnp.int32, (tm, LANES), 1)
        first_half = (lane % SWA_HEAD_DIM) < (SWA_HEAD_DIM // 2)
        for ch in range(chunks):
            t = acc[:, ch * LANES:(ch + 1) * LANES]
            if ch < roped_chunks:
                partner = jnp.where(first_half,
                                    pltpu.roll(t, LANES - SWA_HEAD_DIM // 2, axis=1),
                                    pltpu.roll(t, SWA_HEAD_DIM // 2, axis=1))
                t = t * cos + partner * sin
            o_ref[:, ch * LANES:(ch + 1) * LANES] = t.astype(o_ref.dtype)

    full_tiles = rope_cols // tn
    partial = (rope_cols - full_tiles * tn) // LANES
    pl.when(j < full_tiles)(lambda: store(chunks))
    pl.when(j == full_tiles)(lambda: store(partial))
    pl.when(j > full_tiles)(lambda: store(0))


def _matmul_bias_rope(a, w, bias, cos, sin, rope_cols, tm=1024, tn=1024):
    t, k = a.shape
    n = w.shape[1]
    return pl.pallas_call(
        functools.partial(_mm_rope_kernel, rope_cols=rope_cols),
        grid=(t // tm, n // tn),
        in_specs=[pl.BlockSpec((tm, k), lambda i, j: (i, 0)),
                  pl.BlockSpec((k, tn), lambda i, j: (0, j)),
                  pl.BlockSpec((1, tn), lambda i, j: (0, j)),
                  pl.BlockSpec((tm, LANES), lambda i, j: (i, 0)),
                  pl.BlockSpec((tm, LANES), lambda i, j: (i, 0))],
        out_specs=pl.BlockSpec((tm, tn), lambda i, j: (i, j)),
        out_shape=jax.ShapeDtypeStruct((t, n), BF16),
        compiler_params=_cparams("parallel", "arbitrary"),
        name="swa_in_proj",
    )(a, w, bias.reshape(1, n), cos, sin)


def _mm_res_kernel(a_ref, w_ref, r_ref, o_ref):
    o_ref[...] = r_ref[...] + _dot(a_ref[...], w_ref[...])


def _mm2_res_kernel(a1_ref, a2_ref, w1_ref, w2_ref, r_ref, o_ref):
    o_ref[...] = r_ref[...] + _dot(a1_ref[...], w1_ref[...]) + _dot(a2_ref[...], w2_ref[...])


def _matmul_residual(acts, w, layer, res, tm=1024, tn=1024):
    t, n = res.shape
    a_specs = [pl.BlockSpec((tm, a.shape[1]), lambda i, j: (i, 0)) for a in acts]
    w_specs, row = [], 0
    for a in acts:
        k = a.shape[1]
        w_specs.append(_wspec((k, tn), lambda i, j, rb=row // k: (rb, j), layer))
        row += k
    body = _mm_res_kernel if len(acts) == 1 else _mm2_res_kernel
    return pl.pallas_call(
        body,
        grid=(t // tm, n // tn),
        in_specs=a_specs + w_specs + [pl.BlockSpec((tm, tn), lambda i, j: (i, j))],
        out_specs=pl.BlockSpec((tm, tn), lambda i, j: (i, j)),
        out_shape=jax.ShapeDtypeStruct((t, n), F32),
        compiler_params=_cparams("parallel", "arbitrary"),
        name="out_proj_residual",
    )(*acts, *[w for _ in acts], res)


def _ffn_kernel(x_ref, g_ref, w1g_ref, w1u_ref, w2_ref, fg_ref, o_ref, h_scr, *, final_norm):
    k = pl.program_id(1)

    @pl.when(k == 0)
    def _():
        x = x_ref[...]
        h_scr[...] = _rms(x, g_ref[...]).astype(BF16)
        o_ref[...] = x

    h = h_scr[...]
    gate = _dot(h, w1g_ref[...])
    up = _dot(h, w1u_ref[...])
    act = (gate * _sigmoid(gate) * up * 0.5).astype(BF16)
    o_ref[...] += _dot(act, w2_ref[...])

    if final_norm:
        @pl.when(k == pl.num_programs(1) - 1)
        def _():
            o_ref[...] = _rms(o_ref[...], fg_ref[...])


def _ffn(x, gain, w1, w2, layer, final_gain=None, tm=1024, tf=512):
    t, d = x.shape
    dff = w2.shape[1]
    nk = dff // tf
    fg = jnp.ones((d,), F32) if final_gain is None else final_gain
    return pl.pallas_call(
        functools.partial(_ffn_kernel, final_norm=final_gain is not None),
        grid=(t // tm, nk),
        in_specs=[pl.BlockSpec((tm, d), lambda i, k: (i, 0)),
                  pl.BlockSpec((1, d), lambda i, k: (0, 0)),
                  pl.BlockSpec((None, d, tf), lambda i, k: (layer, 0, k)),
                  pl.BlockSpec((None, d, tf), lambda i, k: (layer, 0, nk + k)),
                  pl.BlockSpec((None, tf, d), lambda i, k: (layer, k, 0)),
                  pl.BlockSpec((1, d), lambda i, k: (0, 0))],
        out_specs=pl.BlockSpec((tm, d), lambda i, k: (i, 0)),
        out_shape=jax.ShapeDtypeStruct((t, d), F32),
        scratch_shapes=[pltpu.VMEM((tm, d), BF16)],
        compiler_params=_cparams("parallel", "arbitrary"),
        name="swiglu_ffn",
    )(x, gain.reshape(1, d), w1, w1, w2, fg.reshape(1, d))


def _xa_kernel(x_ref, g_ref, wq_ref, k_ref, v_ref, wo_ref, o_ref):
    x = x_ref[...]
    h = _rms(x, g_ref[...]).astype(BF16)
    q = _dot(h, wq_ref[...]).astype(BF16)
    scale = XA_HEAD_DIM ** -0.5
    outs = []
    for hd in range(XA_HEADS):
        sl = slice(hd * XA_HEAD_DIM, (hd + 1) * XA_HEAD_DIM)
        s = _dot_nt(q[:, sl], k_ref[:, sl]) * scale
        m = jnp.max(s, axis=-1, keepdims=True)
        p = jnp.exp(s - m)
        den = jnp.sum(p, axis=-1, keepdims=True)
        outs.append((_dot(p.astype(BF16), v_ref[:, sl]) / den).astype(BF16))
    o = jnp.concatenate(outs, axis=-1)
    o_ref[...] = x + _dot(o, wo_ref[...])


def _cross_attention(x, gain, w_q, mem_kv, w_o, layer, seq, tm=1024):
    t, d = x.shape
    per_batch = seq // tm
    return pl.pallas_call(
        _xa_kernel,
        grid=(t // tm,),
        in_specs=[pl.BlockSpec((tm, d), lambda i: (i, 0)),
                  pl.BlockSpec((1, d), lambda i: (0, 0)),
                  _wspec((d, XA_WIDTH), lambda i: (0, 0), layer),
                  pl.BlockSpec((MEM_LEN, XA_WIDTH), lambda i: (i // per_batch, 0)),
                  pl.BlockSpec((MEM_LEN, XA_WIDTH), lambda i: (i // per_batch, 1)),
                  _wspec((XA_WIDTH, d), lambda i: (0, 0), layer)],
        out_specs=pl.BlockSpec((tm, d), lambda i: (i, 0)),
        out_shape=jax.ShapeDtypeStruct((t, d), F32),
        compiler_params=_cparams("parallel"),
        name="memory_cross_attention",
    )(x, gain.reshape(1, d), w_q, mem_kv, mem_kv, w_o)


def _ssd_kernel(z_ref, xs_ref, bc_ref, dt_ref, cw_ref, cb_ref, dtb_ref, alog_ref, dskip_ref,
                gn_ref, e_ref, shift_ref, o_ref, prev_scr, h_scr):
    L = SSD_CHUNK
    c = pl.program_id(1)

    @pl.when(c == 0)
    def _():
        prev_scr[...] = jnp.zeros(prev_scr.shape, BF16)
        h_scr[...] = jnp.zeros(h_scr.shape, F32)

    cur = jnp.concatenate([xs_ref[...], bc_ref[...]], axis=1)
    shifted = _dot(shift_ref[...], jnp.concatenate([prev_scr[...], cur], axis=0))
    prev_scr[...] = cur
    conv = cb_ref[...] + cur.astype(F32) * cw_ref[SSD_CONV - 1:SSD_CONV, :]
    for tap in range(SSD_CONV - 1):
        conv = conv + shifted[tap * L:(tap + 1) * L] * cw_ref[tap:tap + 1, :]
    xbc = conv * _sigmoid(conv)

    row = lax.broadcasted_iota(jnp.int32, (L, LANES), 0)
    col = lax.broadcasted_iota(jnp.int32, (L, LANES), 1)
    v = dt_ref[...] + dtb_ref[...]
    dt = jnp.maximum(v, 0.0) + jnp.log1p(jnp.exp(-jnp.abs(v)))
    da = jnp.where(col < SSD_HEADS, dt * (-jnp.exp(alog_ref[...])), 0.0)
    cs = da
    shift = 1
    while shift < L:
        cs = cs + jnp.where(row >= shift, pltpu.roll(cs, shift, axis=0), 0.0)
        shift *= 2
    exp_cs = jnp.exp(cs)
    decay_to_end = jnp.exp(cs[L - 1:L, :] - cs)
    cs_t = cs.T

    expanded = _split_dot(jnp.concatenate([dt, exp_cs, decay_to_end], axis=0), e_ref[...], 2)
    dt_x = expanded[0:L]
    exp_cs_x = expanded[L:2 * L]
    dte_x = expanded[2 * L:3 * L]

    causal = row >= col
    low_half = col < SSD_HEAD_DIM
    gw = SSD_GROUP_WIDTH
    for g in range(SSD_GROUPS):
        gs = slice(g * gw, (g + 1) * gw)
        b_f32 = xbc[:, SSD_D_INNER + g * SSD_STATE:SSD_D_INNER + (g + 1) * SSD_STATE]
        b_g = b_f32.astype(BF16)
        b_t = b_f32.T.astype(BF16)
        c_off = SSD_D_INNER + SSD_GROUPS * SSD_STATE
        c_g = xbc[:, c_off + g * SSD_STATE:c_off + (g + 1) * SSD_STATE].astype(BF16)
        x_g = xbc[:, gs]
        xdt = x_g * dt_x[:, gs]
        xdt_bf = xdt.astype(BF16)
        cb = _dot_nt(c_g, b_g)

        h_in = h_scr[g]
        y_off = _dot(c_g, h_in.astype(BF16)) * exp_cs_x[:, gs]
        new_state = _dot(b_t, (xdt * dte_x[:, gs]).astype(BF16))
        h_scr[g] = h_in * exp_cs_x[L - 1:L, gs] + new_state

        y_parts = []
        heads_per_group = SSD_HEADS // SSD_GROUPS
        for pair in range(heads_per_group // 2):
            x_pair = xdt_bf[:, pair * LANES:(pair + 1) * LANES]
            acc = None
            for par in range(2):
                hd = g * heads_per_group + 2 * pair + par
                seg = cs[:, hd:hd + 1] - cs_t[hd:hd + 1, :]
                decay = jnp.exp(jnp.where(causal, seg, -jnp.inf))
                m = (cb * decay).astype(BF16)
                rhs = jnp.where(low_half if par == 0 else ~low_half, x_pair, jnp.zeros_like(x_pair))
                term = _dot(m, rhs)
                acc = term if acc is None else acc + term
            y_parts.append(acc)
        y = jnp.concatenate(y_parts, axis=-1) + y_off + x_g * dskip_ref[:, gs]
        zg = z_ref[:, gs].astype(F32)
        y = y * (zg * _sigmoid(zg))
        y = y * lax.rsqrt(jnp.mean(y * y, axis=-1, keepdims=True) + EPS) * gn_ref[:, gs]
        o_ref[:, gs] = y.astype(o_ref.dtype)


def _ssd_branch(zx, dt_raw, conv_w, conv_b, dt_bias, a_log, d_skip, gate_norm, bsz, seq):
    t = zx.shape[0]
    L = SSD_CHUNK
    nc = seq // L
    pad = LANES - SSD_HEADS
    expand = _head_expansion(SSD_D_INNER, SSD_HEAD_DIM)
    out_row = np.arange((SSD_CONV - 1) * L)
    src = L + out_row % L - (SSD_CONV - 1) + out_row // L
    shift = jnp.asarray(src[:, None] == np.arange(2 * L)[None, :], BF16)
    row_map = lambda b, c: (b * nc + c, 0)
    const = lambda b, c: (0, 0)
    return pl.pallas_call(
        _ssd_kernel,
        grid=(bsz, nc),
        in_specs=[pl.BlockSpec((L, SSD_D_INNER), row_map),
                  pl.BlockSpec((L, SSD_D_INNER), lambda b, c: (b * nc + c, 1)),
                  pl.BlockSpec((L, 2 * SSD_GROUPS * SSD_STATE), lambda b, c: (b * nc + c, 4)),
                  pl.BlockSpec((L, LANES), row_map),
                  pl.BlockSpec((SSD_CONV, SSD_XBC), const),
                  pl.BlockSpec((1, SSD_XBC), const),
                  pl.BlockSpec((1, LANES), const),
                  pl.BlockSpec((1, LANES), const),
                  pl.BlockSpec((1, SSD_D_INNER), const),
                  pl.BlockSpec((1, SSD_D_INNER), const),
                  pl.BlockSpec((LANES, SSD_D_INNER), const),
                  pl.BlockSpec(((SSD_CONV - 1) * L, 2 * L), const)],
        out_specs=pl.BlockSpec((L, SSD_D_INNER), row_map),
        out_shape=jax.ShapeDtypeStruct((t, SSD_D_INNER), BF16),
        scratch_shapes=[pltpu.VMEM((L, SSD_XBC), BF16),
                        pltpu.VMEM((SSD_GROUPS, SSD_STATE, SSD_GROUP_WIDTH), F32)],
        compiler_params=_cparams("arbitrary", "arbitrary"),
        name="ssd_scan",
    )(zx, zx, zx, dt_raw, conv_w, conv_b.reshape(1, -1),
      jnp.pad(dt_bias, (0, pad)).reshape(1, LANES), jnp.pad(a_log, (0, pad)).reshape(1, LANES),
      jnp.repeat(d_skip, SSD_HEAD_DIM).reshape(1, -1), gate_norm.reshape(1, -1), expand, shift)


def _band_bias(first_block, max_dist):
    qi = lax.broadcasted_iota(jnp.int32, (ATT_BLOCK, 2 * ATT_BLOCK), 0)
    kj = lax.broadcasted_iota(jnp.int32, (ATT_BLOCK, 2 * ATT_BLOCK), 1)
    lowest = jnp.maximum(qi + (ATT_BLOCK - max_dist), jnp.where(first_block, ATT_BLOCK, 0))
    valid = (kj >= lowest) & (kj <= qi + ATT_BLOCK)
    return jnp.where(valid, 0.0, -jnp.inf).astype(F32)


def _dil_attn_kernel(q_ref, kc_ref, vc_ref, kp_ref, vp_ref, o_ref, lse_ref, *, max_dist):
    scale = DIL_HEAD_DIM ** -0.5
    heads = [slice(hd * DIL_HEAD_DIM, (hd + 1) * DIL_HEAD_DIM) for hd in range(DIL_HEADS)]
    ones = jnp.ones((2 * ATT_BLOCK, LANES), BF16)
    lane = lax.broadcasted_iota(jnp.int32, (ATT_BLOCK, LANES), 1)
    for blk in range(q_ref.shape[0] // ATT_BLOCK):
        cur = slice(blk * ATT_BLOCK, (blk + 1) * ATT_BLOCK)
        prev = slice((blk - 1) * ATT_BLOCK, blk * ATT_BLOCK)
        bias = _band_bias((pl.program_id(1) == 0) if blk == 0 else False, max_dist)
        scores = []
        for sl in heads:
            k_prev = kp_ref[:, sl] if blk == 0 else kc_ref[prev, sl]
            keys = jnp.concatenate([k_prev, kc_ref[cur, sl]], axis=0)
            scores.append(_dot_nt(q_ref[cur, sl], keys) * scale + bias)
        s = jnp.concatenate(scores, axis=0)
        m = jnp.max(s, axis=-1, keepdims=True)
        p = jnp.exp(s - m).astype(BF16)
        lse_tile = jnp.zeros((ATT_BLOCK, LANES), F32)
        for hd, sl in enumerate(heads):
            rows = slice(hd * ATT_BLOCK, (hd + 1) * ATT_BLOCK)
            v_prev = vp_ref[:, sl] if blk == 0 else vc_ref[prev, sl]
            values = jnp.concatenate([v_prev, vc_ref[cur, sl]], axis=0)
            r = _dot(p[rows], jnp.concatenate([values, ones], axis=1))
            den = r[:, LANES:]
            o_ref[cur, sl] = (r[:, :LANES] / den).astype(o_ref.dtype)
            lse_tile = jnp.where(lane == hd, m[rows] + jnp.log(den), lse_tile)
        lse_ref[cur, :] = lse_tile


def _dilated_group_attention(qkv, bsz, seq, dil, n_keys, blocks_per_tile=4):
    t = qkv.shape[0]
    bpt = min(blocks_per_tile, seq // dil // ATT_BLOCK)
    tq = bpt * ATT_BLOCK
    nb = seq // dil // tq
    w = DIL_WIDTH
    cur = lambda part: pl.BlockSpec((tq, w), lambda s, n: (s * nb + n, part))
    prev = lambda part: pl.BlockSpec((ATT_BLOCK, w),
                                     lambda s, n: (jnp.maximum((s * nb + n) * bpt - 1, 0), part))
    return pl.pallas_call(
        functools.partial(_dil_attn_kernel, max_dist=n_keys),
        grid=(bsz * dil, nb),
        in_specs=[cur(0), cur(1), cur(2), prev(1), prev(2)],
        out_specs=[pl.BlockSpec((tq, w), lambda s, n: (s * nb + n, 0)),
                   pl.BlockSpec((tq, LANES), lambda s, n: (s * nb + n, 0))],
        out_shape=[jax.ShapeDtypeStruct((t, w), BF16),
                   jax.ShapeDtypeStruct((t, LANES), F32)],
        compiler_params=_cparams("parallel", "arbitrary"),
        name="dilated_attention",
    )(qkv, qkv, qkv, qkv, qkv)


def _combine_kernel(*refs, dils):
    n = len(dils)
    o_refs, l_refs, perm_refs = refs[:n], refs[n:2 * n], refs[2 * n:-2]
    e_ref, y_ref = refs[-2], refs[-1]
    tm = y_ref.shape[0]
    outs, lses = [], []
    perm_iter = iter(perm_refs)
    for o_ref, l_ref, dil in zip(o_refs, l_refs, dils):
        if dil == 1:
            outs.append(o_ref[...].astype(F32))
            lses.append(l_ref[...])
        else:
            ungroup = next(perm_iter)[...]
            outs.append(_dot(ungroup, o_ref[...].reshape(tm, -1)))
            lse = l_ref[...].reshape(tm, LANES)
            acc, rem = None, lse
            for _ in range(3):
                piece = rem.astype(BF16)
                term = _dot(ungroup, piece)
                acc = term if acc is None else acc + term
                rem = rem - piece.astype(F32)
            lses.append(acc)
    m = functools.reduce(jnp.maximum, lses)
    ws = [jnp.exp(l - m) for l in lses]
    den = functools.reduce(lambda a, b: a + b, ws)
    alpha = jnp.concatenate([w / den for w in ws], axis=0)
    ax = _split_dot(alpha, e_ref[...], 2)
    y = None
    for g, o in enumerate(outs):
        term = ax[g * tm:(g + 1) * tm] * o
        y = term if y is None else y + term
    y_ref[...] = y.astype(y_ref.dtype)


def _combine_groups(outs, lses, bsz, seq, dils, tm=512):
    t, w = outs[0].shape
    tiles = seq // tm
    expand = _head_expansion(w, DIL_HEAD_DIM)

    def spec(dil, width):
        if dil == 1:
            return pl.BlockSpec((tm, width), lambda b, i: (b * tiles + i, 0))
        return pl.BlockSpec((None, dil, tm // dil, width), lambda b, i: (b, 0, i, 0))

    def view(a, dil):
        return a if dil == 1 else a.reshape(bsz, dil, seq // dil, a.shape[-1])

    perms = [_group_permutation(tm, dil, inverse=True) for dil in dils if dil != 1]
    return pl.pallas_call(
        functools.partial(_combine_kernel, dils=dils),
        grid=(bsz, tiles),
        in_specs=[spec(dil, w) for dil in dils] + [spec(dil, LANES) for dil in dils]
                 + [pl.BlockSpec((tm, tm), lambda b, i: (0, 0)) for _ in perms]
                 + [pl.BlockSpec((LANES, w), lambda b, i: (0, 0))],
        out_specs=pl.BlockSpec((tm, w), lambda b, i: (b * tiles + i, 0)),
        out_shape=jax.ShapeDtypeStruct((t, w), BF16),
        compiler_params=_cparams("parallel", "parallel"),
        name="dilated_combine",
    )(*[view(o, dil) for o, dil in zip(outs, dils)], *[view(l, dil) for l, dil in zip(lses, dils)],
      *perms, expand)


def _swa_kernel(sink_ref, q_ref, kc_ref, vc_ref, kp_ref, vp_ref, o_ref):
    q_per_kv = SWA_Q_HEADS // SWA_KV_HEADS
    chunks = q_per_kv // 2
    rows = chunks * ATT_BLOCK
    lane = lax.broadcasted_iota(jnp.int32, (2 * ATT_BLOCK, LANES), 1).astype(F32).astype(BF16)
    key_row = lax.broadcasted_iota(jnp.int32, (2 * ATT_BLOCK, LANES), 0).astype(F32).astype(BF16)
    halves = (lane < SWA_HEAD_DIM, lane >= SWA_HEAD_DIM)
    zero = jnp.zeros((2 * ATT_BLOCK, LANES), BF16)
    one = jnp.ones((2 * ATT_BLOCK, LANES), BF16)
    sink_slot = lax.broadcasted_iota(jnp.int32, (ATT_BLOCK, 2 * ATT_BLOCK), 1) == 0
    scale = SWA_HEAD_DIM ** -0.5

    for blk in range(q_ref.shape[0] // ATT_BLOCK):
        cur = slice(blk * ATT_BLOCK, (blk + 1) * ATT_BLOCK)
        prev = slice((blk - 1) * ATT_BLOCK, blk * ATT_BLOCK)
        bias = _band_bias((pl.program_id(1) == 0) if blk == 0 else False, SWA_WINDOW - 1)
        scores = []
        for g in range(SWA_KV_HEADS):
            gl = slice(g * LANES, (g + 1) * LANES)
            k_prev = kp_ref[:, gl] if blk == 0 else kc_ref[prev, gl]
            keys = jnp.concatenate([k_prev, kc_ref[cur, gl]], axis=0) * scale
            q = jnp.concatenate([q_ref[cur, (g * chunks + c) * LANES:(g * chunks + c + 1) * LANES]
                                 for c in range(chunks)], axis=0)
            for par in range(2):
                qk = _dot_nt(q, jnp.where(halves[par], keys, zero))
                for c in range(chunks):
                    sink = sink_ref[g * q_per_kv + 2 * c + par]
                    scores.append(jnp.where(sink_slot, sink, qk[c * ATT_BLOCK:(c + 1) * ATT_BLOCK] + bias))
        s = jnp.concatenate(scores, axis=0)
        m = jnp.max(s, axis=-1, keepdims=True)
        p = jnp.exp(s - m).astype(BF16)

        for g in range(SWA_KV_HEADS):
            gl = slice(g * LANES, (g + 1) * LANES)
            v_prev = vp_ref[:, gl] if blk == 0 else vc_ref[prev, gl]
            values = jnp.concatenate([v_prev, vc_ref[cur, gl]], axis=0)
            values = jnp.where(key_row < 1, zero, values)
            r = None
            for par in range(2):
                rhs = jnp.concatenate([jnp.where(halves[par], values, zero),
                                       jnp.where(halves[par], one, zero)], axis=1)
                base = (2 * g + par) * rows
                term = _dot(p[base:base + rows], rhs)
                r = term if r is None else r + term
            o = r[:, :LANES] / r[:, LANES:]
            for c in range(chunks):
                col = (g * chunks + c) * LANES
                o_ref[cur, col:col + LANES] = o[c * ATT_BLOCK:(c + 1) * ATT_BLOCK].astype(o_ref.dtype)


def _swa_attention(qkv, sinks, bsz, seq, blocks_per_tile=2):
    t = qkv.shape[0]
    tq = blocks_per_tile * ATT_BLOCK
    nb = seq // tq
    kvw = 2 * ODD_KV
    q_col = ODD_Q // kvw
    prev_block = lambda b, n: jnp.maximum((b * nb + n) * blocks_per_tile - 1, 0)
    return pl.pallas_call(
        _swa_kernel,
        grid=(bsz, nb),
        in_specs=[pl.BlockSpec(memory_space=pltpu.SMEM),
                  pl.BlockSpec((tq, ODD_Q), lambda b, n: (b * nb + n, 0)),
                  pl.BlockSpec((tq, kvw), lambda b, n: (b * nb + n, q_col)),
                  pl.BlockSpec((tq, kvw), lambda b, n: (b * nb + n, q_col + 1)),
                  pl.BlockSpec((ATT_BLOCK, kvw), lambda b, n: (prev_block(b, n), q_col)),
                  pl.BlockSpec((ATT_BLOCK, kvw), lambda b, n: (prev_block(b, n), q_col + 1))],
        out_specs=pl.BlockSpec((tq, ODD_Q), lambda b, n: (b * nb + n, 0)),
        out_shape=jax.ShapeDtypeStruct((t, ODD_Q), BF16),
        compiler_params=_cparams("parallel", "arbitrary"),
        name="swa_sink_attention",
    )(sinks, qkv, qkv, qkv, qkv, qkv)


def _rope_table_kernel(pos_ref, freq_ref, sign_ref, cos_ref, sin_ref):
    ang = pos_ref[...] * freq_ref[...]
    cos_ref[...] = jnp.cos(ang)
    sin_ref[...] = jnp.sin(ang) * sign_ref[...]


def _rope_tables(positions, tm=1024):
    t = positions.size
    half = SWA_HEAD_DIM // 2
    inv_freq = ROPE_THETA ** (-jnp.arange(half, dtype=F32) / half)
    lane = jnp.arange(LANES)
    freq = inv_freq[lane % half].reshape(1, LANES)
    sign = jnp.where((lane % SWA_HEAD_DIM) < half, -1.0, 1.0).astype(F32).reshape(1, LANES)
    pos = positions.astype(F32).reshape(t, 1)
    row = pl.BlockSpec((tm, LANES), lambda i: (i, 0))
    const = pl.BlockSpec((1, LANES), lambda i: (0, 0))
    return pl.pallas_call(
        _rope_table_kernel,
        grid=(t // tm,),
        in_specs=[pl.BlockSpec((tm, 1), lambda i: (i, 0)), const, const],
        out_specs=[row, row],
        out_shape=[jax.ShapeDtypeStruct((t, LANES), F32)] * 2,
        compiler_params=_cparams("parallel"),
        name="rope_tables",
    )(pos, freq, sign)


def _even_mixer(x, mix_gain, bsz, seq, layer, w_in, w_qkv, conv_w, conv_b, dt_bias, a_log, d_skip, gate_norm,
                w_out):
    zx_end = SSD_D_INNER + SSD_XBC
    dils = tuple(dilation for _, dilation in DIL_PATTERNS)
    h, grouped = _rmsnorm_grouped(x, mix_gain, bsz, seq, [d for d in dils if d != 1])
    grouped = iter(grouped)
    hs = [h if d == 1 else next(grouped) for d in dils]

    zx, dt_raw = _matmul_with_dt(h, w_in, layer, zx_end)
    y_a = _ssd_branch(zx, dt_raw, conv_w, conv_b, dt_bias, a_log, d_skip, gate_norm, bsz, seq)

    n_groups = len(DIL_PATTERNS)
    tn = 1024
    per_part = DIL_WIDTH // tn
    outs, lses = [], []
    for g, (window, dilation) in enumerate(DIL_PATTERNS):
        col_block = lambda j, g=g: ((j // per_part) * n_groups + g) * per_part + j % per_part
        qkv = _matmul(hs[g], w_qkv, BF16, tn=tn, n_cols=3 * DIL_WIDTH, col_block=col_block, layer=layer,
                      name="dilated_qkv_proj")
        o, lse = _dilated_group_attention(qkv, bsz, seq, dilation, window // dilation)
        outs.append(o)
        lses.append(lse)
    y_b = _combine_groups(outs, lses, bsz, seq, dils)
    return _matmul_residual([y_a, y_b], w_out, layer, x)


def _odd_mixer(x, h, bsz, seq, layer, cos, sin, w_in, b_in, sinks, w_out):
    def doubled(cols):
        c = cols.reshape(cols.shape[:-1] + (SWA_KV_HEADS, SWA_HEAD_DIM))
        return jnp.concatenate([c, c], axis=-1).reshape(cols.shape[:-1] + (2 * ODD_KV,))

    w = jnp.concatenate([w_in[:, :ODD_Q], doubled(w_in[:, ODD_Q:ODD_Q + ODD_KV]),
                         doubled(w_in[:, ODD_Q + ODD_KV:])], axis=1).astype(BF16)
    b = jnp.concatenate([b_in[:ODD_Q], doubled(b_in[ODD_Q:ODD_Q + ODD_KV]), doubled(b_in[ODD_Q + ODD_KV:])])
    qkv = _matmul_bias_rope(h, w, b, cos, sin, rope_cols=ODD_Q + 2 * ODD_KV)
    o = _swa_attention(qkv, sinks, bsz, seq)
    return _matmul_residual([o], w_out, layer, x)


def kernel(x, mem, positions, ffn1_norm, ffn1_w1, ffn1_w2, mix_norm, even_w_in, even_conv_w, even_conv_b,
           even_dt_bias, even_a_log, even_d_skip, even_gate_norm, even_w_out, odd_w_in, odd_b_in, odd_sinks,
           odd_w_out, mem_norm, mem_w_kv, xa_norm, xa_w_q, xa_w_o, ffn2_norm, ffn2_w1, ffn2_w2, final_norm):
    bsz, seq, d = x.shape
    depth = ffn1_norm.shape[0]
    bf = lambda a: a.astype(BF16)
    x = x.reshape(bsz * seq, d)

    ffn1_w1, ffn1_w2, ffn2_w1, ffn2_w2 = bf(ffn1_w1), bf(ffn1_w2), bf(ffn2_w1), bf(ffn2_w2)
    xa_w_q, xa_w_o = bf(xa_w_q), bf(xa_w_o)
    even_qkv = bf(even_w_in[:, :, SSD_D_INNER + SSD_XBC + SSD_HEADS:])
    even_w_in, even_w_out, odd_w_out = bf(even_w_in), bf(even_w_out), bf(odd_w_out)

    mem_h = _rmsnorm(mem.reshape(bsz * MEM_LEN, d), mem_norm, BF16)
    mem_kv = _matmul(mem_h, bf(mem_w_kv), BF16, name="memory_kv_proj")
    cos, sin = _rope_tables(positions)

    for i in range(depth):
        j = i // 2
        x = _ffn(x, ffn1_norm[i], ffn1_w1, ffn1_w2, i)
        if i % 2 == 0:
            x = _even_mixer(x, mix_norm[i], bsz, seq, j, even_w_in, even_qkv, even_conv_w[j], even_conv_b[j],
                            even_dt_bias[j], even_a_log[j], even_d_skip[j], even_gate_norm[j], even_w_out)
        else:
            h = _rmsnorm(x, mix_norm[i], BF16)
            x = _odd_mixer(x, h, bsz, seq, j, cos, sin, odd_w_in[j], odd_b_in[j], odd_sinks[j], odd_w_out)
        x = _cross_attention(x, xa_norm[i], xa_w_q, mem_kv, xa_w_o, i, seq)
        x = _ffn(x, ffn2_norm[i], ffn2_w1, ffn2_w2, i, final_gain=final_norm if i == depth - 1 else None)
    return x.reshape(bsz, seq, d)
```

```python
import functools
import math

import jax
import jax.numpy as jnp
import numpy as np
from jax import lax
from jax.experimental import pallas as pl
from jax.experimental.pallas import tpu as pltpu

F32 = jnp.float32
BF16 = jnp.bfloat16

EPS = 1e-5
MEM_LEN = 256

SSD_HEADS = 32
SSD_HEAD_DIM = 64
SSD_D_INNER = SSD_HEADS * SSD_HEAD_DIM
SSD_GROUPS = 4
SSD_STATE = 128
SSD_CONV = 4
SSD_CHUNK = 128
SSD_XBC = SSD_D_INNER + 2 * SSD_GROUPS * SSD_STATE
SSD_GROUP_WIDTH = SSD_D_INNER // SSD_GROUPS

DIL_PATTERNS = ((128, 1), (512, 4), (2048, 16))
DIL_HEADS = 8
DIL_HEAD_DIM = 128
DIL_WIDTH = DIL_HEADS * DIL_HEAD_DIM

SWA_Q_HEADS = 32
SWA_KV_HEADS = 4
SWA_HEAD_DIM = 64
SWA_WINDOW = 128
ROPE_THETA = 150000.0
ODD_Q = SWA_Q_HEADS * SWA_HEAD_DIM
ODD_KV = SWA_KV_HEADS * SWA_HEAD_DIM

XA_HEADS = 4
XA_HEAD_DIM = 128
XA_WIDTH = XA_HEADS * XA_HEAD_DIM

LANES = 128
ATT_BLOCK = 128
VMEM_LIMIT = 56 * 1024 * 1024


def _cparams(*sem):
    return pltpu.CompilerParams(dimension_semantics=sem, vmem_limit_bytes=VMEM_LIMIT)


def _rms(x, gain):
    ms = jnp.mean(x * x, axis=-1, keepdims=True)
    return x * lax.rsqrt(ms + EPS) * gain


def _sigmoid(x):
    return 1.0 / (1.0 + jnp.exp(-x))


def _dot(a, b):
    return jnp.dot(a, b, preferred_element_type=F32)


def _dot_nt(a, b):
    return lax.dot_general(a, b, (((1,), (1,)), ((), ())), preferred_element_type=F32)


def _split_dot(v, e, parts):
    out = None
    rem = v
    for _ in range(parts):
        piece = rem.astype(BF16)
        term = _dot(piece, e)
        out = term if out is None else out + term
        rem = rem - piece.astype(F32)
    return out


def _rmsnorm_kernel(x_ref, g_ref, o_ref):
    o_ref[...] = _rms(x_ref[...], g_ref[...]).astype(o_ref.dtype)


def _rmsnorm(x, gain, out_dtype, tm=512):
    t, d = x.shape
    tm = min(tm, t)
    return pl.pallas_call(
        _rmsnorm_kernel,
        grid=(t // tm,),
        in_specs=[pl.BlockSpec((tm, d), lambda i: (i, 0)),
                  pl.BlockSpec((1, d), lambda i: (0, 0))],
        out_specs=pl.BlockSpec((tm, d), lambda i: (i, 0)),
        out_shape=jax.ShapeDtypeStruct((t, d), out_dtype),
        compiler_params=_cparams("parallel"),
        name="rmsnorm",
    )(x, gain.reshape(1, d))


def _group_permutation(tm, dil, inverse=False):
    r = np.arange(tm)
    grouped = (r % dil) * (tm // dil) + r // dil
    g = np.arange(tm)[:, None] == grouped[None, :]
    return jnp.asarray(g.T if inverse else g, BF16)


def _head_expansion(width, head_dim):
    return jnp.asarray(np.arange(LANES)[:, None] == (np.arange(width) // head_dim)[None, :], BF16)


def _rmsnorm_grouped_kernel(x_ref, g_ref, *refs, dils):
    n = len(dils)
    perm_refs, h_ref, grouped_refs = refs[:n], refs[n], refs[n + 1:]
    h = _rms(x_ref[...], g_ref[...]).astype(BF16)
    h_ref[...] = h
    tm = h.shape[0]
    for p_ref, o_ref, dil in zip(perm_refs, grouped_refs, dils):
        hp = _dot(p_ref[...], h).astype(BF16)
        rows = tm // dil
        for c in range(dil):
            o_ref[c] = hp[c * rows:(c + 1) * rows]


def _rmsnorm_grouped(x, gain, bsz, seq, dils, tm=512):
    t, d = x.shape
    tiles = seq // tm
    perms = [_group_permutation(tm, dil) for dil in dils]
    outs = pl.pallas_call(
        functools.partial(_rmsnorm_grouped_kernel, dils=dils),
        grid=(bsz, tiles),
        in_specs=[pl.BlockSpec((tm, d), lambda b, i: (b * tiles + i, 0)),
                  pl.BlockSpec((1, d), lambda b, i: (0, 0))]
                 + [pl.BlockSpec((tm, tm), lambda b, i: (0, 0)) for _ in dils],
        out_specs=[pl.BlockSpec((tm, d), lambda b, i: (b * tiles + i, 0))]
                  + [pl.BlockSpec((None, dil, tm // dil, d), lambda b, i: (b, 0, i, 0)) for dil in dils],
        out_shape=[jax.ShapeDtypeStruct((t, d), BF16)]
                  + [jax.ShapeDtypeStruct((bsz, dil, seq // dil, d), BF16) for dil in dils],
        compiler_params=_cparams("parallel", "parallel"),
        name="rmsnorm_grouped",
    )(x, gain.reshape(1, d), *perms)
    return outs[0], [o.reshape(t, d) for o in outs[1:]]


def _wspec(block, index, layer=None):
    if layer is None:
        return pl.BlockSpec(block, index)
    return pl.BlockSpec((None,) + block, lambda *g: (layer,) + index(*g))


def _mm_kernel(a_ref, w_ref, o_ref):
    o_ref[...] = _dot(a_ref[...], w_ref[...]).astype(o_ref.dtype)


def _matmul(a, w, out_dtype, tm=1024, tn=1024, n_cols=None, col_block=None, layer=None, name="matmul"):
    t, k = a.shape
    n = w.shape[-1] if n_cols is None else n_cols
    tm, tn = min(tm, t), min(tn, n)
    col_block = col_block or (lambda j: j)
    return pl.pallas_call(
        _mm_kernel,
        grid=(t // tm, n // tn),
        in_specs=[pl.BlockSpec((tm, k), lambda i, j: (i, 0)),
                  _wspec((k, tn), lambda i, j: (0, col_block(j)), layer)],
        out_specs=pl.BlockSpec((tm, tn), lambda i, j: (i, j)),
        out_shape=jax.ShapeDtypeStruct((t, n), out_dtype),
        compiler_params=_cparams("parallel", "arbitrary"),
        name=name,
    )(a, w)


def _mm_dt_kernel(a_ref, w_ref, wdt_ref, o_ref, dt_ref):
    a = a_ref[...]
    o_ref[...] = _dot(a, w_ref[...]).astype(o_ref.dtype)

    @pl.when(pl.program_id(1) == 0)
    def _():
        dt_ref[...] = _dot(a, wdt_ref[...])


def _matmul_with_dt(a, w, layer, n_cols, tm=1024, tn=1024):
    t, k = a.shape
    return pl.pallas_call(
        _mm_dt_kernel,
        grid=(t // tm, n_cols // tn),
        in_specs=[pl.BlockSpec((tm, k), lambda i, j: (i, 0)),
                  _wspec((k, tn), lambda i, j: (0, j), layer),
                  _wspec((k, LANES), lambda i, j: (0, n_cols // LANES), layer)],
        out_specs=[pl.BlockSpec((tm, tn), lambda i, j: (i, j)),
                   pl.BlockSpec((tm, LANES), lambda i, j: (i, 0))],
        out_shape=[jax.ShapeDtypeStruct((t, n_cols), BF16),
                   jax.ShapeDtypeStruct((t, LANES), F32)],
        compiler_params=_cparams("parallel", "arbitrary"),
        name="ssd_in_proj",
    )(a, w, w)


def _mm_rope_kernel(a_ref, w_ref, b_ref, cos_ref, sin_ref, swap_ref, o_ref, *, rope_cols):
    j = pl.program_id(1)
    acc = _dot(a_ref[...], w_ref[...]) + b_ref[...]
    tm, tn = acc.shape
    chunks = tn // LANES

    def store(roped_chunks):
        cos = cos_ref[...]
        sin = sin_ref[...]
        swap = swap_ref[...]
        for ch in range(chunks):
            t = acc[:, ch * LANES:(ch + 1) * LANES]
            if ch < roped_chunks:
                t = t * cos + _dot(t.astype(BF16), swap) * sin
            o_ref[:, ch * LANES:(ch + 1) * LANES] = t.astype(o_ref.dtype)

    full_tiles = rope_cols // tn
    partial = (rope_cols - full_tiles * tn) // LANES
    pl.when(j < full_tiles)(lambda: store(chunks))
    pl.when(j == full_tiles)(lambda: store(partial))
    pl.when(j > full_tiles)(lambda: store(0))


def _matmul_bias_rope(a, w, bias, cos, sin, rope_cols, tm=1024, tn=1024):
    t, k = a.shape
    n = w.shape[1]
    lane = np.arange(LANES)
    half = SWA_HEAD_DIM // 2
    partner = np.where(lane % SWA_HEAD_DIM < half, lane + half, lane - half)
    swap = jnp.asarray(lane[:, None] == partner[None, :], BF16)
    return pl.pallas_call(
        functools.partial(_mm_rope_kernel, rope_cols=rope_cols),
        grid=(t // tm, n // tn),
        in_specs=[pl.BlockSpec((tm, k), lambda i, j: (i, 0)),
                  pl.BlockSpec((k, tn), lambda i, j: (0, j)),
                  pl.BlockSpec((1, tn), lambda i, j: (0, j)),
                  pl.BlockSpec((tm, LANES), lambda i, j: (i, 0)),
                  pl.BlockSpec((tm, LANES), lambda i, j: (i, 0)),
                  pl.BlockSpec((LANES, LANES), lambda i, j: (0, 0))],
        out_specs=pl.BlockSpec((tm, tn), lambda i, j: (i, j)),
        out_shape=jax.ShapeDtypeStruct((t, n), BF16),
        compiler_params=_cparams("parallel", "arbitrary"),
        name="swa_in_proj",
    )(a, w, bias.reshape(1, n), cos, sin, swap)


def _mm_res_kernel(a_ref, w_ref, r_ref, o_ref):
    o_ref[...] = r_ref[...] + _dot(a_ref[...], w_ref[...])


def _mm2_res_kernel(a1_ref, a2_ref, w1_ref, w2_ref, r_ref, o_ref):
    o_ref[...] = r_ref[...] + _dot(a1_ref[...], w1_ref[...]) + _dot(a2_ref[...], w2_ref[...])


def _matmul_residual(acts, w, layer, res, tm=1024, tn=1024):
    t, n = res.shape
    a_specs = [pl.BlockSpec((tm, a.shape[1]), lambda i, j: (i, 0)) for a in acts]
    w_specs, row = [], 0
    for a in acts:
        k = a.shape[1]
        w_specs.append(_wspec((k, tn), lambda i, j, rb=row // k: (rb, j), layer))
        row += k
    body = _mm_res_kernel if len(acts) == 1 else _mm2_res_kernel
    return pl.pallas_call(
        body,
        grid=(t // tm, n // tn),
        in_specs=a_specs + w_specs + [pl.BlockSpec((tm, tn), lambda i, j: (i, j))],
        out_specs=pl.BlockSpec((tm, tn), lambda i, j: (i, j)),
        out_shape=jax.ShapeDtypeStruct((t, n), F32),
        compiler_params=_cparams("parallel", "arbitrary"),
        name="out_proj_residual",
    )(*acts, *[w for _ in acts], res)


def _ffn_kernel(x_ref, g_ref, w1g_ref, w1u_ref, w2_ref, fg_ref, o_ref, h_scr, *, final_norm):
    k = pl.program_id(1)

    @pl.when(k == 0)
    def _():
        x = x_ref[...]
        h_scr[...] = _rms(x, g_ref[...]).astype(BF16)
        o_ref[...] = x

    h = h_scr[...]
    gate = _dot(h, w1g_ref[...])
    up = _dot(h, w1u_ref[...])
    act = (gate * _sigmoid(gate) * up * 0.5).astype(BF16)
    o_ref[...] += _dot(act, w2_ref[...])

    if final_norm:
        @pl.when(k == pl.num_programs(1) - 1)
        def _():
            o_ref[...] = _rms(o_ref[...], fg_ref[...])


def _ffn(x, gain, w1, w2, layer, final_gain=None, tm=1024, tf=512):
    t, d = x.shape
    dff = w2.shape[1]
    nk = dff // tf
    fg = jnp.ones((d,), F32) if final_gain is None else final_gain
    return pl.pallas_call(
        functools.partial(_ffn_kernel, final_norm=final_gain is not None),
        grid=(t // tm, nk),
        in_specs=[pl.BlockSpec((tm, d), lambda i, k: (i, 0)),
                  pl.BlockSpec((1, d), lambda i, k: (0, 0)),
                  pl.BlockSpec((None, d, tf), lambda i, k: (layer, 0, k)),
                  pl.BlockSpec((None, d, tf), lambda i, k: (layer, 0, nk + k)),
                  pl.BlockSpec((None, tf, d), lambda i, k: (layer, k, 0)),
                  pl.BlockSpec((1, d), lambda i, k: (0, 0))],
        out_specs=pl.BlockSpec((tm, d), lambda i, k: (i, 0)),
        out_shape=jax.ShapeDtypeStruct((t, d), F32),
        scratch_shapes=[pltpu.VMEM((tm, d), BF16)],
        compiler_params=_cparams("parallel", "arbitrary"),
        name="swiglu_ffn",
    )(x, gain.reshape(1, d), w1, w1, w2, fg.reshape(1, d))


def _xa_kernel(x_ref, g_ref, wq_ref, k_ref, v_ref, wo_ref, o_ref):
    x = x_ref[...]
    h = _rms(x, g_ref[...]).astype(BF16)
    q = _dot(h, wq_ref[...]).astype(BF16)
    scale = XA_HEAD_DIM ** -0.5
    outs = []
    for hd in range(XA_HEADS):
        sl = slice(hd * XA_HEAD_DIM, (hd + 1) * XA_HEAD_DIM)
        s = _dot_nt(q[:, sl], k_ref[:, sl]) * scale
        m = jnp.max(s, axis=-1, keepdims=True)
        p = jnp.exp(s - m)
        den = jnp.sum(p, axis=-1, keepdims=True)
        outs.append((_dot(p.astype(BF16), v_ref[:, sl]) / den).astype(BF16))
    o = jnp.concatenate(outs, axis=-1)
    o_ref[...] = x + _dot(o, wo_ref[...])


def _cross_attention(x, gain, w_q, mem_kv, w_o, layer, seq, tm=1024):
    t, d = x.shape
    per_batch = seq // tm
    return pl.pallas_call(
        _xa_kernel,
        grid=(t // tm,),
        in_specs=[pl.BlockSpec((tm, d), lambda i: (i, 0)),
                  pl.BlockSpec((1, d), lambda i: (0, 0)),
                  _wspec((d, XA_WIDTH), lambda i: (0, 0), layer),
                  pl.BlockSpec((MEM_LEN, XA_WIDTH), lambda i: (i // per_batch, 0)),
                  pl.BlockSpec((MEM_LEN, XA_WIDTH), lambda i: (i // per_batch, 1)),
                  _wspec((XA_WIDTH, d), lambda i: (0, 0), layer)],
        out_specs=pl.BlockSpec((tm, d), lambda i: (i, 0)),
        out_shape=jax.ShapeDtypeStruct((t, d), F32),
        compiler_params=_cparams("parallel"),
        name="memory_cross_attention",
    )(x, gain.reshape(1, d), w_q, mem_kv, mem_kv, w_o)


def _ssd_kernel(z_ref, xs_ref, bc_ref, dt_ref, cw_ref, cb_ref, dtb_ref, alog_ref, dskip_ref,
                gn_ref, e_ref, shift_ref, o_ref, prev_scr, h_scr):
    L = SSD_CHUNK
    c = pl.program_id(1)

    @pl.when(c == 0)
    def _():
        prev_scr[...] = jnp.zeros(prev_scr.shape, BF16)
        h_scr[...] = jnp.zeros(h_scr.shape, F32)

    cur = jnp.concatenate([xs_ref[...], bc_ref[...]], axis=1)
    shifted = _dot(shift_ref[...], jnp.concatenate([prev_scr[...], cur], axis=0))
    prev_scr[...] = cur
    conv = cb_ref[...] + cur.astype(F32) * cw_ref[SSD_CONV - 1:SSD_CONV, :]
    for tap in range(SSD_CONV - 1):
        conv = conv + shifted[tap * L:(tap + 1) * L] * cw_ref[tap:tap + 1, :]
    xbc = conv * _sigmoid(conv)

    row = lax.broadcasted_iota(jnp.int32, (L, LANES), 0)
    col = lax.broadcasted_iota(jnp.int32, (L, LANES), 1)
    v = dt_ref[...] + dtb_ref[...]
    dt = jnp.maximum(v, 0.0) + jnp.log1p(jnp.exp(-jnp.abs(v)))
    da = jnp.where(col < SSD_HEADS, dt * (-jnp.exp(alog_ref[...])), 0.0)
    cs = da
    shift = 1
    while shift < L:
        cs = cs + jnp.where(row >= shift, pltpu.roll(cs, shift, axis=0), 0.0)
        shift *= 2
    exp_cs = jnp.exp(cs)
    decay_to_end = jnp.exp(cs[L - 1:L, :] - cs)
    cs_t = cs.T

    expanded = _split_dot(jnp.concatenate([dt, exp_cs, decay_to_end], axis=0), e_ref[...], 1)
    dt_x = expanded[0:L]
    exp_cs_x = expanded[L:2 * L]
    dte_x = expanded[2 * L:3 * L]

    causal = row >= col
    low_half = col < SSD_HEAD_DIM
    gw = SSD_GROUP_WIDTH
    for g in range(SSD_GROUPS):
        gs = slice(g * gw, (g + 1) * gw)
        b_f32 = xbc[:, SSD_D_INNER + g * SSD_STATE:SSD_D_INNER + (g + 1) * SSD_STATE]
        b_g = b_f32.astype(BF16)
        b_t = b_f32.T.astype(BF16)
        c_off = SSD_D_INNER + SSD_GROUPS * SSD_STATE
        c_g = xbc[:, c_off + g * SSD_STATE:c_off + (g + 1) * SSD_STATE].astype(BF16)
        x_g = xbc[:, gs]
        xdt = x_g * dt_x[:, gs]
        xdt_bf = xdt.astype(BF16)
        cb = _dot_nt(c_g, b_g)

        h_in = h_scr[g]
        y_off = _dot(c_g, h_in.astype(BF16)) * exp_cs_x[:, gs]
        new_state = _dot(b_t, (xdt * dte_x[:, gs]).astype(BF16))
        h_scr[g] = h_in * exp_cs_x[L - 1:L, gs] + new_state

        y_parts = []
        heads_per_group = SSD_HEADS // SSD_GROUPS
        for pair in range(heads_per_group // 2):
            x_pair = xdt_bf[:, pair * LANES:(pair + 1) * LANES]
            acc = None
            for par in range(2):
                hd = g * heads_per_group + 2 * pair + par
                seg = cs[:, hd:hd + 1] - cs_t[hd:hd + 1, :]
                decay = jnp.exp(jnp.where(causal, seg, -jnp.inf))
                m = (cb * decay).astype(BF16)
                rhs = jnp.where(low_half if par == 0 else ~low_half, x_pair, jnp.zeros_like(x_pair))
                term = _dot(m, rhs)
                acc = term if acc is None else acc + term
            y_parts.append(acc)
        y = jnp.concatenate(y_parts, axis=-1) + y_off + x_g * dskip_ref[:, gs]
        zg = z_ref[:, gs].astype(F32)
        y = y * (zg * _sigmoid(zg))
        y = y * lax.rsqrt(jnp.mean(y * y, axis=-1, keepdims=True) + EPS) * gn_ref[:, gs]
        o_ref[:, gs] = y.astype(o_ref.dtype)


def _ssd_branch(zx, dt_raw, conv_w, conv_b, dt_bias, a_log, d_skip, gate_norm, bsz, seq):
    t = zx.shape[0]
    L = SSD_CHUNK
    nc = seq // L
    pad = LANES - SSD_HEADS
    expand = _head_expansion(SSD_D_INNER, SSD_HEAD_DIM)
    out_row = np.arange((SSD_CONV - 1) * L)
    src = L + out_row % L - (SSD_CONV - 1) + out_row // L
    shift = jnp.asarray(src[:, None] == np.arange(2 * L)[None, :], BF16)
    row_map = lambda b, c: (b * nc + c, 0)
    const = lambda b, c: (0, 0)
    return pl.pallas_call(
        _ssd_kernel,
        grid=(bsz, nc),
        in_specs=[pl.BlockSpec((L, SSD_D_INNER), row_map),
                  pl.BlockSpec((L, SSD_D_INNER), lambda b, c: (b * nc + c, 1)),
                  pl.BlockSpec((L, 2 * SSD_GROUPS * SSD_STATE), lambda b, c: (b * nc + c, 4)),
                  pl.BlockSpec((L, LANES), row_map),
                  pl.BlockSpec((SSD_CONV, SSD_XBC), const),
                  pl.BlockSpec((1, SSD_XBC), const),
                  pl.BlockSpec((1, LANES), const),
                  pl.BlockSpec((1, LANES), const),
                  pl.BlockSpec((1, SSD_D_INNER), const),
                  pl.BlockSpec((1, SSD_D_INNER), const),
                  pl.BlockSpec((LANES, SSD_D_INNER), const),
                  pl.BlockSpec(((SSD_CONV - 1) * L, 2 * L), const)],
        out_specs=pl.BlockSpec((L, SSD_D_INNER), row_map),
        out_shape=jax.ShapeDtypeStruct((t, SSD_D_INNER), BF16),
        scratch_shapes=[pltpu.VMEM((L, SSD_XBC), BF16),
                        pltpu.VMEM((SSD_GROUPS, SSD_STATE, SSD_GROUP_WIDTH), F32)],
        compiler_params=_cparams("arbitrary", "arbitrary"),
        name="ssd_scan",
    )(zx, zx, zx, dt_raw, conv_w, conv_b.reshape(1, -1),
      jnp.pad(dt_bias, (0, pad)).reshape(1, LANES), jnp.pad(a_log, (0, pad)).reshape(1, LANES),
      jnp.repeat(d_skip, SSD_HEAD_DIM).reshape(1, -1), gate_norm.reshape(1, -1), expand, shift)


def _band_bias(first_block, max_dist):
    qi = lax.broadcasted_iota(jnp.int32, (ATT_BLOCK, 2 * ATT_BLOCK), 0)
    kj = lax.broadcasted_iota(jnp.int32, (ATT_BLOCK, 2 * ATT_BLOCK), 1)
    lowest = jnp.maximum(qi + (ATT_BLOCK - max_dist), jnp.where(first_block, ATT_BLOCK, 0))
    valid = (kj >= lowest) & (kj <= qi + ATT_BLOCK)
    return jnp.where(valid, 0.0, -jnp.inf).astype(F32)


def _dil_attn_kernel(q_ref, kc_ref, vc_ref, kp_ref, vp_ref, o_ref, lse_ref, *, max_dist):
    scale = DIL_HEAD_DIM ** -0.5
    heads = [slice(hd * DIL_HEAD_DIM, (hd + 1) * DIL_HEAD_DIM) for hd in range(DIL_HEADS)]
    ones = jnp.ones((2 * ATT_BLOCK, LANES), BF16)
    lane = lax.broadcasted_iota(jnp.int32, (ATT_BLOCK, LANES), 1)
    for blk in range(q_ref.shape[0] // ATT_BLOCK):
        cur = slice(blk * ATT_BLOCK, (blk + 1) * ATT_BLOCK)
        prev = slice((blk - 1) * ATT_BLOCK, blk * ATT_BLOCK)
        bias = _band_bias((pl.program_id(1) == 0) if blk == 0 else False, max_dist)
        scores = []
        for sl in heads:
            k_prev = kp_ref[:, sl] if blk == 0 else kc_ref[prev, sl]
            keys = jnp.concatenate([k_prev, kc_ref[cur, sl]], axis=0)
            scores.append(_dot_nt(q_ref[cur, sl], keys) * scale + bias)
        s = jnp.concatenate(scores, axis=0)
        m = jnp.max(s, axis=-1, keepdims=True)
        p = jnp.exp(s - m).astype(BF16)
        lse_tile = jnp.zeros((ATT_BLOCK, LANES), F32)
        for hd, sl in enumerate(heads):
            rows = slice(hd * ATT_BLOCK, (hd + 1) * ATT_BLOCK)
            v_prev = vp_ref[:, sl] if blk == 0 else vc_ref[prev, sl]
            values = jnp.concatenate([v_prev, vc_ref[cur, sl]], axis=0)
            r = _dot(p[rows], jnp.concatenate([values, ones], axis=1))
            den = r[:, LANES:]
            o_ref[cur, sl] = (r[:, :LANES] / den).astype(o_ref.dtype)
            lse_tile = jnp.where(lane == hd, m[rows] + jnp.log(den), lse_tile)
        lse_ref[cur, :] = lse_tile


def _dilated_group_attention(qkv, bsz, seq, dil, n_keys, blocks_per_tile=8):
    t = qkv.shape[0]
    bpt = min(blocks_per_tile, seq // dil // ATT_BLOCK)
    tq = bpt * ATT_BLOCK
    nb = seq // dil // tq
    w = DIL_WIDTH
    cur = lambda part: pl.BlockSpec((tq, w), lambda s, n: (s * nb + n, part))
    prev = lambda part: pl.BlockSpec((ATT_BLOCK, w),
                                     lambda s, n: (jnp.maximum((s * nb + n) * bpt - 1, 0), part))
    return pl.pallas_call(
        functools.partial(_dil_attn_kernel, max_dist=n_keys),
        grid=(bsz * dil, nb),
        in_specs=[cur(0), cur(1), cur(2), prev(1), prev(2)],
        out_specs=[pl.BlockSpec((tq, w), lambda s, n: (s * nb + n, 0)),
                   pl.BlockSpec((tq, LANES), lambda s, n: (s * nb + n, 0))],
        out_shape=[jax.ShapeDtypeStruct((t, w), BF16),
                   jax.ShapeDtypeStruct((t, LANES), F32)],
        compiler_params=_cparams("parallel", "arbitrary"),
        name="dilated_attention",
    )(qkv, qkv, qkv, qkv, qkv)


def _combine_kernel(*refs, dils):
    n = len(dils)
    o_refs, l_refs, perm_refs = refs[:n], refs[n:2 * n], refs[2 * n:-2]
    e_ref, y_ref = refs[-2], refs[-1]
    tm = y_ref.shape[0]
    outs, lses = [], []
    perm_iter = iter(perm_refs)
    for o_ref, l_ref, dil in zip(o_refs, l_refs, dils):
        if dil == 1:
            outs.append(o_ref[...].astype(F32))
            lses.append(l_ref[...])
        else:
            ungroup = next(perm_iter)[...]
            outs.append(_dot(ungroup, o_ref[...].reshape(tm, -1)))
            lse = l_ref[...].reshape(tm, LANES)
            acc, rem = None, lse
            for _ in range(3):
                piece = rem.astype(BF16)
                term = _dot(ungroup, piece)
                acc = term if acc is None else acc + term
                rem = rem - piece.astype(F32)
            lses.append(acc)
    m = functools.reduce(jnp.maximum, lses)
    ws = [jnp.exp(l - m) for l in lses]
    den = functools.reduce(lambda a, b: a + b, ws)
    alpha = jnp.concatenate([w / den for w in ws], axis=0)
    ax = _split_dot(alpha, e_ref[...], 2)
    y = None
    for g, o in enumerate(outs):
        term = ax[g * tm:(g + 1) * tm] * o
        y = term if y is None else y + term
    y_ref[...] = y.astype(y_ref.dtype)


def _combine_groups(outs, lses, bsz, seq, dils, tm=512):
    t, w = outs[0].shape
    tiles = seq // tm
    expand = _head_expansion(w, DIL_HEAD_DIM)

    def spec(dil, width):
        if dil == 1:
            return pl.BlockSpec((tm, width), lambda b, i: (b * tiles + i, 0))
        return pl.BlockSpec((None, dil, tm // dil, width), lambda b, i: (b, 0, i, 0))

    def view(a, dil):
        return a if dil == 1 else a.reshape(bsz, dil, seq // dil, a.shape[-1])

    perms = [_group_permutation(tm, dil, inverse=True) for dil in dils if dil != 1]
    return pl.pallas_call(
        functools.partial(_combine_kernel, dils=dils),
        grid=(bsz, tiles),
        in_specs=[spec(dil, w) for dil in dils] + [spec(dil, LANES) for dil in dils]
                 + [pl.BlockSpec((tm, tm), lambda b, i: (0, 0)) for _ in perms]
                 + [pl.BlockSpec((LANES, w), lambda b, i: (0, 0))],
        out_specs=pl.BlockSpec((tm, w), lambda b, i: (b * tiles + i, 0)),
        out_shape=jax.ShapeDtypeStruct((t, w), BF16),
        compiler_params=_cparams("parallel", "parallel"),
        name="dilated_combine",
    )(*[view(o, dil) for o, dil in zip(outs, dils)], *[view(l, dil) for l, dil in zip(lses, dils)],
      *perms, expand)


def _swa_kernel(sink_ref, q_ref, kc_ref, vc_ref, kp_ref, vp_ref, o_ref):
    q_per_kv = SWA_Q_HEADS // SWA_KV_HEADS
    chunks = q_per_kv // 2
    rows = chunks * ATT_BLOCK
    lane = lax.broadcasted_iota(jnp.int32, (2 * ATT_BLOCK, LANES), 1).astype(F32).astype(BF16)
    key_row = lax.broadcasted_iota(jnp.int32, (2 * ATT_BLOCK, LANES), 0).astype(F32).astype(BF16)
    halves = (lane < SWA_HEAD_DIM, lane >= SWA_HEAD_DIM)
    zero = jnp.zeros((2 * ATT_BLOCK, LANES), BF16)
    one = jnp.ones((2 * ATT_BLOCK, LANES), BF16)
    sink_slot = lax.broadcasted_iota(jnp.int32, (ATT_BLOCK, 2 * ATT_BLOCK), 1) == 0
    scale = SWA_HEAD_DIM ** -0.5

    for blk in range(q_ref.shape[0] // ATT_BLOCK):
        cur = slice(blk * ATT_BLOCK, (blk + 1) * ATT_BLOCK)
        prev = slice((blk - 1) * ATT_BLOCK, blk * ATT_BLOCK)
        bias = _band_bias((pl.program_id(1) == 0) if blk == 0 else False, SWA_WINDOW - 1)
        scores = []
        for g in range(SWA_KV_HEADS):
            gl = slice(g * LANES, (g + 1) * LANES)
            k_prev = kp_ref[:, gl] if blk == 0 else kc_ref[prev, gl]
            keys = jnp.concatenate([k_prev, kc_ref[cur, gl]], axis=0) * scale
            q = jnp.concatenate([q_ref[cur, (g * chunks + c) * LANES:(g * chunks + c + 1) * LANES]
                                 for c in range(chunks)], axis=0)
            for par in range(2):
                qk = _dot_nt(q, jnp.where(halves[par], keys, zero))
                for c in range(chunks):
                    sink = sink_ref[g * q_per_kv + 2 * c + par]
                    scores.append(jnp.where(sink_slot, sink, qk[c * ATT_BLOCK:(c + 1) * ATT_BLOCK] + bias))
        s = jnp.concatenate(scores, axis=0)
        m = jnp.max(s, axis=-1, keepdims=True)
        p = jnp.exp(s - m).astype(BF16)

        for g in range(SWA_KV_HEADS):
            gl = slice(g * LANES, (g + 1) * LANES)
            v_prev = vp_ref[:, gl] if blk == 0 else vc_ref[prev, gl]
            values = jnp.concatenate([v_prev, vc_ref[cur, gl]], axis=0)
            values = jnp.where(key_row < 1, zero, values)
            r = None
            for par in range(2):
                rhs = jnp.concatenate([jnp.where(halves[par], values, zero),
                                       jnp.where(halves[par], one, zero)], axis=1)
                base = (2 * g + par) * rows
                term = _dot(p[base:base + rows], rhs)
                r = term if r is None else r + term
            o = r[:, :LANES] / r[:, LANES:]
            for c in range(chunks):
                col = (g * chunks + c) * LANES
                o_ref[cur, col:col + LANES] = o[c * ATT_BLOCK:(c + 1) * ATT_BLOCK].astype(o_ref.dtype)


def _swa_attention(qkv, sinks, bsz, seq, blocks_per_tile=4):
    t = qkv.shape[0]
    tq = blocks_per_tile * ATT_BLOCK
    nb = seq // tq
    kvw = 2 * ODD_KV
    q_col = ODD_Q // kvw
    prev_block = lambda b, n: jnp.maximum((b * nb + n) * blocks_per_tile - 1, 0)
    return pl.pallas_call(
        _swa_kernel,
        grid=(bsz, nb),
        in_specs=[pl.BlockSpec(memory_space=pltpu.SMEM),
                  pl.BlockSpec((tq, ODD_Q), lambda b, n: (b * nb + n, 0)),
                  pl.BlockSpec((tq, kvw), lambda b, n: (b * nb + n, q_col)),
                  pl.BlockSpec((tq, kvw), lambda b, n: (b * nb + n, q_col + 1)),
                  pl.BlockSpec((ATT_BLOCK, kvw), lambda b, n: (prev_block(b, n), q_col)),
                  pl.BlockSpec((ATT_BLOCK, kvw), lambda b, n: (prev_block(b, n), q_col + 1))],
        out_specs=pl.BlockSpec((tq, ODD_Q), lambda b, n: (b * nb + n, 0)),
        out_shape=jax.ShapeDtypeStruct((t, ODD_Q), BF16),
        compiler_params=_cparams("parallel", "arbitrary"),
        name="swa_sink_attention",
    )(sinks, qkv, qkv, qkv, qkv, qkv)


def _rope_table_kernel(pos_ref, freq_ref, sign_ref, cos_ref, sin_ref):
    ang = pos_ref[...] * freq_ref[...]
    cos_ref[...] = jnp.cos(ang)
    sin_ref[...] = jnp.sin(ang) * sign_ref[...]


def _rope_tables(positions, tm=1024):
    t = positions.size
    half = SWA_HEAD_DIM // 2
    inv_freq = ROPE_THETA ** (-jnp.arange(half, dtype=F32) / half)
    lane = jnp.arange(LANES)
    freq = inv_freq[lane % half].reshape(1, LANES)
    sign = jnp.where((lane % SWA_HEAD_DIM) < half, -1.0, 1.0).astype(F32).reshape(1, LANES)
    pos = positions.astype(F32).reshape(t, 1)
    row = pl.BlockSpec((tm, LANES), lambda i: (i, 0))
    const = pl.BlockSpec((1, LANES), lambda i: (0, 0))
    return pl.pallas_call(
        _rope_table_kernel,
        grid=(t // tm,),
        in_specs=[pl.BlockSpec((tm, 1), lambda i: (i, 0)), const, const],
        out_specs=[row, row],
        out_shape=[jax.ShapeDtypeStruct((t, LANES), F32)] * 2,
        compiler_params=_cparams("parallel"),
        name="rope_tables",
    )(pos, freq, sign)


def _even_mixer(x, mix_gain, bsz, seq, layer, w_in, w_qkv, conv_w, conv_b, dt_bias, a_log, d_skip, gate_norm,
                w_out):
    zx_end = SSD_D_INNER + SSD_XBC
    dils = tuple(dilation for _, dilation in DIL_PATTERNS)
    h, grouped = _rmsnorm_grouped(x, mix_gain, bsz, seq, [d for d in dils if d != 1])
    grouped = iter(grouped)
    hs = [h if d == 1 else next(grouped) for d in dils]

    zx, dt_raw = _matmul_with_dt(h, w_in, layer, zx_end)
    y_a = _ssd_branch(zx, dt_raw, conv_w, conv_b, dt_bias, a_log, d_skip, gate_norm, bsz, seq)

    n_groups = len(DIL_PATTERNS)
    tn = 1024
    per_part = DIL_WIDTH // tn
    outs, lses = [], []
    for g, (window, dilation) in enumerate(DIL_PATTERNS):
        col_block = lambda j, g=g: ((j // per_part) * n_groups + g) * per_part + j % per_part
        qkv = _matmul(hs[g], w_qkv, BF16, tn=tn, n_cols=3 * DIL_WIDTH, col_block=col_block, layer=layer,
                      name="dilated_qkv_proj")
        o, lse = _dilated_group_attention(qkv, bsz, seq, dilation, window // dilation)
        outs.append(o)
        lses.append(lse)
    y_b = _combine_groups(outs, lses, bsz, seq, dils)
    return _matmul_residual([y_a, y_b], w_out, layer, x)


def _odd_mixer(x, h, bsz, seq, layer, cos, sin, w_in, b_in, sinks, w_out):
    def doubled(cols):
        c = cols.reshape(cols.shape[:-1] + (SWA_KV_HEADS, SWA_HEAD_DIM))
        return jnp.concatenate([c, c], axis=-1).reshape(cols.shape[:-1] + (2 * ODD_KV,))

    w = jnp.concatenate([w_in[:, :ODD_Q], doubled(w_in[:, ODD_Q:ODD_Q + ODD_KV]),
                         doubled(w_in[:, ODD_Q + ODD_KV:])], axis=1).astype(BF16)
    b = jnp.concatenate([b_in[:ODD_Q], doubled(b_in[ODD_Q:ODD_Q + ODD_KV]), doubled(b_in[ODD_Q + ODD_KV:])])
    qkv = _matmul_bias_rope(h, w, b, cos, sin, rope_cols=ODD_Q + 2 * ODD_KV)
    o = _swa_attention(qkv, sinks, bsz, seq)
    return _matmul_residual([o], w_out, layer, x)


def kernel(x, mem, positions, ffn1_norm, ffn1_w1, ffn1_w2, mix_norm, even_w_in, even_conv_w, even_conv_b,
           even_dt_bias, even_a_log, even_d_skip, even_gate_norm, even_w_out, odd_w_in, odd_b_in, odd_sinks,
           odd_w_out, mem_norm, mem_w_kv, xa_norm, xa_w_q, xa_w_o, ffn2_norm, ffn2_w1, ffn2_w2, final_norm):
    bsz, seq, d = x.shape
    depth = ffn1_norm.shape[0]
    bf = lambda a: a.astype(BF16)
    x = x.reshape(bsz * seq, d)

    ffn1_w1, ffn1_w2, ffn2_w1, ffn2_w2 = bf(ffn1_w1), bf(ffn1_w2), bf(ffn2_w1), bf(ffn2_w2)
    xa_w_q, xa_w_o = bf(xa_w_q), bf(xa_w_o)
    even_qkv = bf(even_w_in[:, :, SSD_D_INNER + SSD_XBC + SSD_HEADS:])
    even_w_in, even_w_out, odd_w_out = bf(even_w_in), bf(even_w_out), bf(odd_w_out)

    mem_h = _rmsnorm(mem.reshape(bsz * MEM_LEN, d), mem_norm, BF16)
    mem_kv = _matmul(mem_h, bf(mem_w_kv), BF16, name="memory_kv_proj")
    cos, sin = _rope_tables(positions)

    for i in range(depth):
        j = i // 2
        x = _ffn(x, ffn1_norm[i], ffn1_w1, ffn1_w2, i)
        if i % 2 == 0:
            x = _even_mixer(x, mix_norm[i], bsz, seq, j, even_w_in, even_qkv, even_conv_w[j], even_conv_b[j],
                            even_dt_bias[j], even_a_log[j], even_d_skip[j], even_gate_norm[j], even_w_out)
        else:
            h = _rmsnorm(x, mix_norm[i], BF16)
            x = _odd_mixer(x, h, bsz, seq, j, cos, sin, odd_w_in[j], odd_b_in[j], odd_sinks[j], odd_w_out)
        x = _cross_attention(x, xa_norm[i], xa_w_q, mem_kv, xa_w_o, i, seq)
        x = _ffn(x, ffn2_norm[i], ffn2_w1, ffn2_w2, i, final_gain=final_norm if i == depth - 1 else None)
    return x.reshape(bsz, seq, d)
```

```python
import functools
import math

import jax
import jax.numpy as jnp
import numpy as np
from jax import lax
from jax.experimental import pallas as pl
from jax.experimental.pallas import tpu as pltpu

F32 = jnp.float32
BF16 = jnp.bfloat16

EPS = 1e-5
MEM_LEN = 256

SSD_HEADS = 32
SSD_HEAD_DIM = 64
SSD_D_INNER = SSD_HEADS * SSD_HEAD_DIM
SSD_GROUPS = 4
SSD_STATE = 128
SSD_CONV = 4
SSD_CHUNK = 128
SSD_XBC = SSD_D_INNER + 2 * SSD_GROUPS * SSD_STATE
SSD_GROUP_WIDTH = SSD_D_INNER // SSD_GROUPS

DIL_PATTERNS = ((128, 1), (512, 4), (2048, 16))
DIL_HEADS = 8
DIL_HEAD_DIM = 128
DIL_WIDTH = DIL_HEADS * DIL_HEAD_DIM

SWA_Q_HEADS = 32
SWA_KV_HEADS = 4
SWA_HEAD_DIM = 64
SWA_WINDOW = 128
ROPE_THETA = 150000.0
ODD_Q = SWA_Q_HEADS * SWA_HEAD_DIM
ODD_KV = SWA_KV_HEADS * SWA_HEAD_DIM

XA_HEADS = 4
XA_HEAD_DIM = 128
XA_WIDTH = XA_HEADS * XA_HEAD_DIM

LANES = 128
ATT_BLOCK = 128
VMEM_LIMIT = 56 * 1024 * 1024


def _cparams(*sem):
    return pltpu.CompilerParams(dimension_semantics=sem, vmem_limit_bytes=VMEM_LIMIT)


def _rms(x, gain):
    ms = jnp.mean(x * x, axis=-1, keepdims=True)
    return x * lax.rsqrt(ms + EPS) * gain


def _sigmoid(x):
    return 1.0 / (1.0 + jnp.exp(-x))


def _dot(a, b):
    return jnp.dot(a, b, preferred_element_type=F32)


def _dot_nt(a, b):
    return lax.dot_general(a, b, (((1,), (1,)), ((), ())), preferred_element_type=F32)


def _split_dot(v, e, parts):
    out = None
    rem = v
    for _ in range(parts):
        piece = rem.astype(BF16)
        term = _dot(piece, e)
        out = term if out is None else out + term
        rem = rem - piece.astype(F32)
    return out


def _rmsnorm_kernel(x_ref, g_ref, o_ref):
    o_ref[...] = _rms(x_ref[...], g_ref[...]).astype(o_ref.dtype)


def _rmsnorm(x, gain, out_dtype, tm=512):
    t, d = x.shape
    tm = min(tm, t)
    return pl.pallas_call(
        _rmsnorm_kernel,
        grid=(t // tm,),
        in_specs=[pl.BlockSpec((tm, d), lambda i: (i, 0)),
                  pl.BlockSpec((1, d), lambda i: (0, 0))],
        out_specs=pl.BlockSpec((tm, d), lambda i: (i, 0)),
        out_shape=jax.ShapeDtypeStruct((t, d), out_dtype),
        compiler_params=_cparams("parallel"),
        name="rmsnorm",
    )(x, gain.reshape(1, d))


def _group_permutation(tm, dil, inverse=False):
    r = np.arange(tm)
    grouped = (r % dil) * (tm // dil) + r // dil
    g = np.arange(tm)[:, None] == grouped[None, :]
    return jnp.asarray(g.T if inverse else g, BF16)


def _head_expansion(width, head_dim):
    return jnp.asarray(np.arange(LANES)[:, None] == (np.arange(width) // head_dim)[None, :], BF16)


def _rmsnorm_grouped_kernel(x_ref, g_ref, *refs, dils):
    n = len(dils)
    perm_refs, h_ref, grouped_refs = refs[:n], refs[n], refs[n + 1:]
    h = _rms(x_ref[...], g_ref[...]).astype(BF16)
    h_ref[...] = h
    tm = h.shape[0]
    for p_ref, o_ref, dil in zip(perm_refs, grouped_refs, dils):
        hp = _dot(p_ref[...], h).astype(BF16)
        rows = tm // dil
        for c in range(dil):
            o_ref[c] = hp[c * rows:(c + 1) * rows]


def _rmsnorm_grouped(x, gain, bsz, seq, dils, tm=512):
    t, d = x.shape
    tiles = seq // tm
    perms = [_group_permutation(tm, dil) for dil in dils]
    outs = pl.pallas_call(
        functools.partial(_rmsnorm_grouped_kernel, dils=dils),
        grid=(bsz, tiles),
        in_specs=[pl.BlockSpec((tm, d), lambda b, i: (b * tiles + i, 0)),
                  pl.BlockSpec((1, d), lambda b, i: (0, 0))]
                 + [pl.BlockSpec((tm, tm), lambda b, i: (0, 0)) for _ in dils],
        out_specs=[pl.BlockSpec((tm, d), lambda b, i: (b * tiles + i, 0))]
                  + [pl.BlockSpec((None, dil, tm // dil, d), lambda b, i: (b, 0, i, 0)) for dil in dils],
        out_shape=[jax.ShapeDtypeStruct((t, d), BF16)]
                  + [jax.ShapeDtypeStruct((bsz, dil, seq // dil, d), BF16) for dil in dils],
        compiler_params=_cparams("parallel", "parallel"),
        name="rmsnorm_grouped",
    )(x, gain.reshape(1, d), *perms)
    return outs[0], [o.reshape(t, d) for o in outs[1:]]


def _wspec(block, index, layer=None):
    if layer is None:
        return pl.BlockSpec(block, index)
    return pl.BlockSpec((None,) + block, lambda *g: (layer,) + index(*g))


def _mm_kernel(a_ref, w_ref, o_ref):
    o_ref[...] = _dot(a_ref[...], w_ref[...]).astype(o_ref.dtype)


def _matmul(a, w, out_dtype, tm=2048, tn=1024, n_cols=None, col_block=None, layer=None, name="matmul"):
    t, k = a.shape
    n = w.shape[-1] if n_cols is None else n_cols
    tm, tn = min(tm, t), min(tn, n)
    col_block = col_block or (lambda j: j)
    return pl.pallas_call(
        _mm_kernel,
        grid=(t // tm, n // tn),
        in_specs=[pl.BlockSpec((tm, k), lambda i, j: (i, 0)),
                  _wspec((k, tn), lambda i, j: (0, col_block(j)), layer)],
        out_specs=pl.BlockSpec((tm, tn), lambda i, j: (i, j)),
        out_shape=jax.ShapeDtypeStruct((t, n), out_dtype),
        compiler_params=_cparams("parallel", "arbitrary"),
        name=name,
    )(a, w)


def _mm_dt_kernel(a_ref, w_ref, wdt_ref, o_ref, dt_ref):
    a = a_ref[...]
    o_ref[...] = _dot(a, w_ref[...]).astype(o_ref.dtype)

    @pl.when(pl.program_id(1) == 0)
    def _():
        dt_ref[...] = _dot(a, wdt_ref[...])


def _matmul_with_dt(a, w, layer, n_cols, tm=2048, tn=1024):
    t, k = a.shape
    return pl.pallas_call(
        _mm_dt_kernel,
        grid=(t // tm, n_cols // tn),
        in_specs=[pl.BlockSpec((tm, k), lambda i, j: (i, 0)),
                  _wspec((k, tn), lambda i, j: (0, j), layer),
                  _wspec((k, LANES), lambda i, j: (0, n_cols // LANES), layer)],
        out_specs=[pl.BlockSpec((tm, tn), lambda i, j: (i, j)),
                   pl.BlockSpec((tm, LANES), lambda i, j: (i, 0))],
        out_shape=[jax.ShapeDtypeStruct((t, n_cols), BF16),
                   jax.ShapeDtypeStruct((t, LANES), F32)],
        compiler_params=_cparams("parallel", "arbitrary"),
        name="ssd_in_proj",
    )(a, w, w)


def _mm_rope_kernel(a_ref, w_ref, b_ref, cos_ref, sin_ref, swap_ref, o_ref, *, rope_cols):
    j = pl.program_id(1)
    acc = _dot(a_ref[...], w_ref[...]) + b_ref[...]
    tm, tn = acc.shape
    chunks = tn // LANES

    def store(roped_chunks):
        cos = cos_ref[...]
        sin = sin_ref[...]
        swap = swap_ref[...]
        for ch in range(chunks):
            t = acc[:, ch * LANES:(ch + 1) * LANES]
            if ch < roped_chunks:
                t = t * cos + _dot(t.astype(BF16), swap) * sin
            o_ref[:, ch * LANES:(ch + 1) * LANES] = t.astype(o_ref.dtype)

    full_tiles = rope_cols // tn
    partial = (rope_cols - full_tiles * tn) // LANES
    pl.when(j < full_tiles)(lambda: store(chunks))
    pl.when(j == full_tiles)(lambda: store(partial))
    pl.when(j > full_tiles)(lambda: store(0))


def _matmul_bias_rope(a, w, bias, cos, sin, rope_cols, tm=1024, tn=1024):
    t, k = a.shape
    n = w.shape[1]
    lane = np.arange(LANES)
    half = SWA_HEAD_DIM // 2
    partner = np.where(lane % SWA_HEAD_DIM < half, lane + half, lane - half)
    swap = jnp.asarray(lane[:, None] == partner[None, :], BF16)
    return pl.pallas_call(
        functools.partial(_mm_rope_kernel, rope_cols=rope_cols),
        grid=(t // tm, n // tn),
        in_specs=[pl.BlockSpec((tm, k), lambda i, j: (i, 0)),
                  pl.BlockSpec((k, tn), lambda i, j: (0, j)),
                  pl.BlockSpec((1, tn), lambda i, j: (0, j)),
                  pl.BlockSpec((tm, LANES), lambda i, j: (i, 0)),
                  pl.BlockSpec((tm, LANES), lambda i, j: (i, 0)),
                  pl.BlockSpec((LANES, LANES), lambda i, j: (0, 0))],
        out_specs=pl.BlockSpec((tm, tn), lambda i, j: (i, j)),
        out_shape=jax.ShapeDtypeStruct((t, n), BF16),
        compiler_params=_cparams("parallel", "arbitrary"),
        name="swa_in_proj",
    )(a, w, bias.reshape(1, n), cos, sin, swap)


def _mm_res_kernel(a_ref, w_ref, r_ref, o_ref):
    o_ref[...] = r_ref[...] + _dot(a_ref[...], w_ref[...])


def _mm2_res_kernel(a1_ref, a2_ref, w1_ref, w2_ref, r_ref, o_ref):
    o_ref[...] = r_ref[...] + _dot(a1_ref[...], w1_ref[...]) + _dot(a2_ref[...], w2_ref[...])


def _matmul_residual(acts, w, layer, res, tm=1024, tn=1024):
    t, n = res.shape
    a_specs = [pl.BlockSpec((tm, a.shape[1]), lambda i, j: (i, 0)) for a in acts]
    w_specs, row = [], 0
    for a in acts:
        k = a.shape[1]
        w_specs.append(_wspec((k, tn), lambda i, j, rb=row // k: (rb, j), layer))
        row += k
    body = _mm_res_kernel if len(acts) == 1 else _mm2_res_kernel
    return pl.pallas_call(
        body,
        grid=(t // tm, n // tn),
        in_specs=a_specs + w_specs + [pl.BlockSpec((tm, tn), lambda i, j: (i, j))],
        out_specs=pl.BlockSpec((tm, tn), lambda i, j: (i, j)),
        out_shape=jax.ShapeDtypeStruct((t, n), F32),
        compiler_params=_cparams("parallel", "arbitrary"),
        name="out_proj_residual",
    )(*acts, *[w for _ in acts], res)


def _ffn_kernel(x_ref, g_ref, w1g_ref, w1u_ref, w2_ref, fg_ref, o_ref, h_scr, *, final_norm):
    k = pl.program_id(1)

    @pl.when(k == 0)
    def _():
        x = x_ref[...]
        h_scr[...] = _rms(x, g_ref[...]).astype(BF16)
        o_ref[...] = x

    h = h_scr[...]
    gate = _dot(h, w1g_ref[...])
    up = _dot(h, w1u_ref[...])
    act = (gate * _sigmoid(gate) * up * 0.5).astype(BF16)
    o_ref[...] += _dot(act, w2_ref[...])

    if final_norm:
        @pl.when(k == pl.num_programs(1) - 1)
        def _():
            o_ref[...] = _rms(o_ref[...], fg_ref[...])


def _ffn(x, gain, w1, w2, layer, final_gain=None, tm=1024, tf=512):
    t, d = x.shape
    dff = w2.shape[1]
    nk = dff // tf
    fg = jnp.ones((d,), F32) if final_gain is None else final_gain
    return pl.pallas_call(
        functools.partial(_ffn_kernel, final_norm=final_gain is not None),
        grid=(t // tm, nk),
        in_specs=[pl.BlockSpec((tm, d), lambda i, k: (i, 0)),
                  pl.BlockSpec((1, d), lambda i, k: (0, 0)),
                  pl.BlockSpec((None, d, tf), lambda i, k: (layer, 0, k)),
                  pl.BlockSpec((None, d, tf), lambda i, k: (layer, 0, nk + k)),
                  pl.BlockSpec((None, tf, d), lambda i, k: (layer, k, 0)),
                  pl.BlockSpec((1, d), lambda i, k: (0, 0))],
        out_specs=pl.BlockSpec((tm, d), lambda i, k: (i, 0)),
        out_shape=jax.ShapeDtypeStruct((t, d), F32),
        scratch_shapes=[pltpu.VMEM((tm, d), BF16)],
        compiler_params=_cparams("parallel", "arbitrary"),
        name="swiglu_ffn",
    )(x, gain.reshape(1, d), w1, w1, w2, fg.reshape(1, d))


def _xa_kernel(x_ref, g_ref, wq_ref, k_ref, v_ref, wo_ref, o_ref):
    x = x_ref[...]
    h = _rms(x, g_ref[...]).astype(BF16)
    q = _dot(h, wq_ref[...]).astype(BF16)
    scale = XA_HEAD_DIM ** -0.5
    outs = []
    for hd in range(XA_HEADS):
        sl = slice(hd * XA_HEAD_DIM, (hd + 1) * XA_HEAD_DIM)
        s = _dot_nt(q[:, sl], k_ref[:, sl]) * scale
        m = jnp.max(s, axis=-1, keepdims=True)
        p = jnp.exp(s - m)
        den = jnp.sum(p, axis=-1, keepdims=True)
        outs.append((_dot(p.astype(BF16), v_ref[:, sl]) / den).astype(BF16))
    o = jnp.concatenate(outs, axis=-1)
    o_ref[...] = x + _dot(o, wo_ref[...])


def _cross_attention(x, gain, w_q, mem_kv, w_o, layer, seq, tm=1024):
    t, d = x.shape
    per_batch = seq // tm
    return pl.pallas_call(
        _xa_kernel,
        grid=(t // tm,),
        in_specs=[pl.BlockSpec((tm, d), lambda i: (i, 0)),
                  pl.BlockSpec((1, d), lambda i: (0, 0)),
                  _wspec((d, XA_WIDTH), lambda i: (0, 0), layer),
                  pl.BlockSpec((MEM_LEN, XA_WIDTH), lambda i: (i // per_batch, 0)),
                  pl.BlockSpec((MEM_LEN, XA_WIDTH), lambda i: (i // per_batch, 1)),
                  _wspec((XA_WIDTH, d), lambda i: (0, 0), layer)],
        out_specs=pl.BlockSpec((tm, d), lambda i: (i, 0)),
        out_shape=jax.ShapeDtypeStruct((t, d), F32),
        compiler_params=_cparams("parallel"),
        name="memory_cross_attention",
    )(x, gain.reshape(1, d), w_q, mem_kv, mem_kv, w_o)


def _ssd_kernel(z_ref, xs_ref, bc_ref, dt_ref, cw_ref, cb_ref, dtb_ref, alog_ref, dskip_ref,
                gn_ref, e_ref, shift_ref, o_ref, prev_scr, h_scr):
    L = SSD_CHUNK
    c = pl.program_id(1)

    @pl.when(c == 0)
    def _():
        prev_scr[...] = jnp.zeros(prev_scr.shape, BF16)
        h_scr[...] = jnp.zeros(h_scr.shape, F32)

    cur = jnp.concatenate([xs_ref[...], bc_ref[...]], axis=1)
    shifted = _dot(shift_ref[...], jnp.concatenate([prev_scr[...], cur], axis=0))
    prev_scr[...] = cur
    conv = cb_ref[...] + cur.astype(F32) * cw_ref[SSD_CONV - 1:SSD_CONV, :]
    for tap in range(SSD_CONV - 1):
        conv = conv + shifted[tap * L:(tap + 1) * L] * cw_ref[tap:tap + 1, :]
    xbc = conv * _sigmoid(conv)

    row = lax.broadcasted_iota(jnp.int32, (L, LANES), 0)
    col = lax.broadcasted_iota(jnp.int32, (L, LANES), 1)
    v = dt_ref[...] + dtb_ref[...]
    dt = jnp.maximum(v, 0.0) + jnp.log1p(jnp.exp(-jnp.abs(v)))
    da = jnp.where(col < SSD_HEADS, dt * (-jnp.exp(alog_ref[...])), 0.0)
    cs = da
    shift = 1
    while shift < L:
        cs = cs + jnp.where(row >= shift, pltpu.roll(cs, shift, axis=0), 0.0)
        shift *= 2
    exp_cs = jnp.exp(cs)
    decay_to_end = jnp.exp(cs[L - 1:L, :] - cs)
    cs_t = cs.T

    expanded = _split_dot(jnp.concatenate([dt, exp_cs, decay_to_end], axis=0), e_ref[...], 1)
    dt_x = expanded[0:L]
    exp_cs_x = expanded[L:2 * L]
    dte_x = expanded[2 * L:3 * L]

    causal = row >= col
    low_half = col < SSD_HEAD_DIM
    gw = SSD_GROUP_WIDTH
    for g in range(SSD_GROUPS):
        gs = slice(g * gw, (g + 1) * gw)
        b_f32 = xbc[:, SSD_D_INNER + g * SSD_STATE:SSD_D_INNER + (g + 1) * SSD_STATE]
        b_g = b_f32.astype(BF16)
        b_t = b_f32.T.astype(BF16)
        c_off = SSD_D_INNER + SSD_GROUPS * SSD_STATE
        c_g = xbc[:, c_off + g * SSD_STATE:c_off + (g + 1) * SSD_STATE].astype(BF16)
        x_g = xbc[:, gs]
        xdt = x_g * dt_x[:, gs]
        xdt_bf = xdt.astype(BF16)
        cb = _dot_nt(c_g, b_g)

        h_in = h_scr[g]
        y_off = _dot(c_g, h_in.astype(BF16)) * exp_cs_x[:, gs]
        new_state = _dot(b_t, (xdt * dte_x[:, gs]).astype(BF16))
        h_scr[g] = h_in * exp_cs_x[L - 1:L, gs] + new_state

        y_parts = []
        heads_per_group = SSD_HEADS // SSD_GROUPS
        for pair in range(heads_per_group // 2):
            x_pair = xdt_bf[:, pair * LANES:(pair + 1) * LANES]
            acc = None
            for par in range(2):
                hd = g * heads_per_group + 2 * pair + par
                seg = cs[:, hd:hd + 1] - cs_t[hd:hd + 1, :]
                decay = jnp.exp(jnp.where(causal, seg, -jnp.inf))
                m = (cb * decay).astype(BF16)
                rhs = jnp.where(low_half if par == 0 else ~low_half, x_pair, jnp.zeros_like(x_pair))
                term = _dot(m, rhs)
                acc = term if acc is None else acc + term
            y_parts.append(acc)
        y = jnp.concatenate(y_parts, axis=-1) + y_off + x_g * dskip_ref[:, gs]
        zg = z_ref[:, gs].astype(F32)
        y = y * (zg * _sigmoid(zg))
        y = y * lax.rsqrt(jnp.mean(y * y, axis=-1, keepdims=True) + EPS) * gn_ref[:, gs]
        o_ref[:, gs] = y.astype(o_ref.dtype)


def _ssd_branch(zx, dt_raw, conv_w, conv_b, dt_bias, a_log, d_skip, gate_norm, bsz, seq):
    t = zx.shape[0]
    L = SSD_CHUNK
    nc = seq // L
    pad = LANES - SSD_HEADS
    expand = _head_expansion(SSD_D_INNER, SSD_HEAD_DIM)
    out_row = np.arange((SSD_CONV - 1) * L)
    src = L + out_row % L - (SSD_CONV - 1) + out_row // L
    shift = jnp.asarray(src[:, None] == np.arange(2 * L)[None, :], BF16)
    row_map = lambda b, c: (b * nc + c, 0)
    const = lambda b, c: (0, 0)
    return pl.pallas_call(
        _ssd_kernel,
        grid=(bsz, nc),
        in_specs=[pl.BlockSpec((L, SSD_D_INNER), row_map),
                  pl.BlockSpec((L, SSD_D_INNER), lambda b, c: (b * nc + c, 1)),
                  pl.BlockSpec((L, 2 * SSD_GROUPS * SSD_STATE), lambda b, c: (b * nc + c, 4)),
                  pl.BlockSpec((L, LANES), row_map),
                  pl.BlockSpec((SSD_CONV, SSD_XBC), const),
                  pl.BlockSpec((1, SSD_XBC), const),
                  pl.BlockSpec((1, LANES), const),
                  pl.BlockSpec((1, LANES), const),
                  pl.BlockSpec((1, SSD_D_INNER), const),
                  pl.BlockSpec((1, SSD_D_INNER), const),
                  pl.BlockSpec((LANES, SSD_D_INNER), const),
                  pl.BlockSpec(((SSD_CONV - 1) * L, 2 * L), const)],
        out_specs=pl.BlockSpec((L, SSD_D_INNER), row_map),
        out_shape=jax.ShapeDtypeStruct((t, SSD_D_INNER), BF16),
        scratch_shapes=[pltpu.VMEM((L, SSD_XBC), BF16),
                        pltpu.VMEM((SSD_GROUPS, SSD_STATE, SSD_GROUP_WIDTH), F32)],
        compiler_params=_cparams("arbitrary", "arbitrary"),
        name="ssd_scan",
    )(zx, zx, zx, dt_raw, conv_w, conv_b.reshape(1, -1),
      jnp.pad(dt_bias, (0, pad)).reshape(1, LANES), jnp.pad(a_log, (0, pad)).reshape(1, LANES),
      jnp.repeat(d_skip, SSD_HEAD_DIM).reshape(1, -1), gate_norm.reshape(1, -1), expand, shift)


def _band_bias(first_block, max_dist):
    qi = lax.broadcasted_iota(jnp.int32, (ATT_BLOCK, 2 * ATT_BLOCK), 0)
    kj = lax.broadcasted_iota(jnp.int32, (ATT_BLOCK, 2 * ATT_BLOCK), 1)
    lowest = jnp.maximum(qi + (ATT_BLOCK - max_dist), jnp.where(first_block, ATT_BLOCK, 0))
    valid = (kj >= lowest) & (kj <= qi + ATT_BLOCK)
    return jnp.where(valid, 0.0, -jnp.inf).astype(F32)


def _dil_attn_kernel(q_ref, kc_ref, vc_ref, kp_ref, vp_ref, o_ref, lse_ref, *, max_dist):
    scale = DIL_HEAD_DIM ** -0.5
    heads = [slice(hd * DIL_HEAD_DIM, (hd + 1) * DIL_HEAD_DIM) for hd in range(DIL_HEADS)]
    ones = jnp.ones((2 * ATT_BLOCK, LANES), BF16)
    lane = lax.broadcasted_iota(jnp.int32, (ATT_BLOCK, LANES), 1)
    for blk in range(q_ref.shape[0] // ATT_BLOCK):
        cur = slice(blk * ATT_BLOCK, (blk + 1) * ATT_BLOCK)
        prev = slice((blk - 1) * ATT_BLOCK, blk * ATT_BLOCK)
        bias = _band_bias((pl.program_id(1) == 0) if blk == 0 else False, max_dist)
        scores = []
        for sl in heads:
            k_prev = kp_ref[:, sl] if blk == 0 else kc_ref[prev, sl]
            keys = jnp.concatenate([k_prev, kc_ref[cur, sl]], axis=0)
            scores.append(_dot_nt(q_ref[cur, sl], keys) * scale + bias)
        s = jnp.concatenate(scores, axis=0)
        m = jnp.max(s, axis=-1, keepdims=True)
        p = jnp.exp(s - m).astype(BF16)
        lse_tile = jnp.zeros((ATT_BLOCK, LANES), F32)
        for hd, sl in enumerate(heads):
            rows = slice(hd * ATT_BLOCK, (hd + 1) * ATT_BLOCK)
            v_prev = vp_ref[:, sl] if blk == 0 else vc_ref[prev, sl]
            values = jnp.concatenate([v_prev, vc_ref[cur, sl]], axis=0)
            r = _dot(p[rows], jnp.concatenate([values, ones], axis=1))
            den = r[:, LANES:]
            o_ref[cur, sl] = (r[:, :LANES] / den).astype(o_ref.dtype)
            lse_tile = jnp.where(lane == hd, m[rows] + jnp.log(den), lse_tile)
        lse_ref[cur, :] = lse_tile


def _dilated_group_attention(qkv, bsz, seq, dil, n_keys, blocks_per_tile=8):
    t = qkv.shape[0]
    bpt = min(blocks_per_tile, seq // dil // ATT_BLOCK)
    tq = bpt * ATT_BLOCK
    nb = seq // dil // tq
    w = DIL_WIDTH
    cur = lambda part: pl.BlockSpec((tq, w), lambda s, n: (s * nb + n, part))
    prev = lambda part: pl.BlockSpec((ATT_BLOCK, w),
                                     lambda s, n: (jnp.maximum((s * nb + n) * bpt - 1, 0), part))
    return pl.pallas_call(
        functools.partial(_dil_attn_kernel, max_dist=n_keys),
        grid=(bsz * dil, nb),
        in_specs=[cur(0), cur(1), cur(2), prev(1), prev(2)],
        out_specs=[pl.BlockSpec((tq, w), lambda s, n: (s * nb + n, 0)),
                   pl.BlockSpec((tq, LANES), lambda s, n: (s * nb + n, 0))],
        out_shape=[jax.ShapeDtypeStruct((t, w), BF16),
                   jax.ShapeDtypeStruct((t, LANES), F32)],
        compiler_params=_cparams("parallel", "arbitrary"),
        name="dilated_attention",
    )(qkv, qkv, qkv, qkv, qkv)


def _combine_kernel(*refs, dils):
    n = len(dils)
    o_refs, l_refs, perm_refs = refs[:n], refs[n:2 * n], refs[2 * n:-2]
    e_ref, y_ref = refs[-2], refs[-1]
    tm = y_ref.shape[0]
    outs, lses = [], []
    perm_iter = iter(perm_refs)
    for o_ref, l_ref, dil in zip(o_refs, l_refs, dils):
        if dil == 1:
            outs.append(o_ref[...].astype(F32))
            lses.append(l_ref[...])
        else:
            ungroup = next(perm_iter)[...]
            outs.append(_dot(ungroup, o_ref[...].reshape(tm, -1)))
            lse = l_ref[...].reshape(tm, LANES)
            acc, rem = None, lse
            for _ in range(3):
                piece = rem.astype(BF16)
                term = _dot(ungroup, piece)
                acc = term if acc is None else acc + term
                rem = rem - piece.astype(F32)
            lses.append(acc)
    m = functools.reduce(jnp.maximum, lses)
    ws = [jnp.exp(l - m) for l in lses]
    den = functools.reduce(lambda a, b: a + b, ws)
    alpha = jnp.concatenate([w / den for w in ws], axis=0)
    ax = _split_dot(alpha, e_ref[...], 2)
    y = None
    for g, o in enumerate(outs):
        term = ax[g * tm:(g + 1) * tm] * o
        y = term if y is None else y + term
    y_ref[...] = y.astype(y_ref.dtype)


def _combine_groups(outs, lses, bsz, seq, dils, tm=512):
    t, w = outs[0].shape
    tiles = seq // tm
    expand = _head_expansion(w, DIL_HEAD_DIM)

    def spec(dil, width):
        if dil == 1:
            return pl.BlockSpec((tm, width), lambda b, i: (b * tiles + i, 0))
        return pl.BlockSpec((None, dil, tm // dil, width), lambda b, i: (b, 0, i, 0))

    def view(a, dil):
        return a if dil == 1 else a.reshape(bsz, dil, seq // dil, a.shape[-1])

    perms = [_group_permutation(tm, dil, inverse=True) for dil in dils if dil != 1]
    return pl.pallas_call(
        functools.partial(_combine_kernel, dils=dils),
        grid=(bsz, tiles),
        in_specs=[spec(dil, w) for dil in dils] + [spec(dil, LANES) for dil in dils]
                 + [pl.BlockSpec((tm, tm), lambda b, i: (0, 0)) for _ in perms]
                 + [pl.BlockSpec((LANES, w), lambda b, i: (0, 0))],
        out_specs=pl.BlockSpec((tm, w), lambda b, i: (b * tiles + i, 0)),
        out_shape=jax.ShapeDtypeStruct((t, w), BF16),
        compiler_params=_cparams("parallel", "parallel"),
        name="dilated_combine",
    )(*[view(o, dil) for o, dil in zip(outs, dils)], *[view(l, dil) for l, dil in zip(lses, dils)],
      *perms, expand)


def _swa_kernel(sink_ref, q_ref, kc_ref, vc_ref, kp_ref, vp_ref, o_ref):
    q_per_kv = SWA_Q_HEADS // SWA_KV_HEADS
    chunks = q_per_kv // 2
    rows = chunks * ATT_BLOCK
    lane = lax.broadcasted_iota(jnp.int32, (2 * ATT_BLOCK, LANES), 1).astype(F32).astype(BF16)
    key_row = lax.broadcasted_iota(jnp.int32, (2 * ATT_BLOCK, LANES), 0).astype(F32).astype(BF16)
    halves = (lane < SWA_HEAD_DIM, lane >= SWA_HEAD_DIM)
    zero = jnp.zeros((2 * ATT_BLOCK, LANES), BF16)
    one = jnp.ones((2 * ATT_BLOCK, LANES), BF16)
    sink_slot = lax.broadcasted_iota(jnp.int32, (ATT_BLOCK, 2 * ATT_BLOCK), 1) == 0
    scale = SWA_HEAD_DIM ** -0.5

    for blk in range(q_ref.shape[0] // ATT_BLOCK):
        cur = slice(blk * ATT_BLOCK, (blk + 1) * ATT_BLOCK)
        prev = slice((blk - 1) * ATT_BLOCK, blk * ATT_BLOCK)
        bias = _band_bias((pl.program_id(1) == 0) if blk == 0 else False, SWA_WINDOW - 1)
        scores = []
        for g in range(SWA_KV_HEADS):
            gl = slice(g * LANES, (g + 1) * LANES)
            k_prev = kp_ref[:, gl] if blk == 0 else kc_ref[prev, gl]
            keys = jnp.concatenate([k_prev, kc_ref[cur, gl]], axis=0) * scale
            q = jnp.concatenate([q_ref[cur, (g * chunks + c) * LANES:(g * chunks + c + 1) * LANES]
                                 for c in range(chunks)], axis=0)
            for par in range(2):
                qk = _dot_nt(q, jnp.where(halves[par], keys, zero))
                for c in range(chunks):
                    sink = sink_ref[g * q_per_kv + 2 * c + par]
                    scores.append(jnp.where(sink_slot, sink, qk[c * ATT_BLOCK:(c + 1) * ATT_BLOCK] + bias))
        s = jnp.concatenate(scores, axis=0)
        m = jnp.max(s, axis=-1, keepdims=True)
        p = jnp.exp(s - m).astype(BF16)

        for g in range(SWA_KV_HEADS):
            gl = slice(g * LANES, (g + 1) * LANES)
            v_prev = vp_ref[:, gl] if blk == 0 else vc_ref[prev, gl]
            values = jnp.concatenate([v_prev, vc_ref[cur, gl]], axis=0)
            values = jnp.where(key_row < 1, zero, values)
            r = None
            for par in range(2):
                rhs = jnp.concatenate([jnp.where(halves[par], values, zero),
                                       jnp.where(halves[par], one, zero)], axis=1)
                base = (2 * g + par) * rows
                term = _dot(p[base:base + rows], rhs)
                r = term if r is None else r + term
            o = r[:, :LANES] / r[:, LANES:]
            for c in range(chunks):
                col = (g * chunks + c) * LANES
                o_ref[cur, col:col + LANES] = o[c * ATT_BLOCK:(c + 1) * ATT_BLOCK].astype(o_ref.dtype)


def _swa_attention(qkv, sinks, bsz, seq, blocks_per_tile=4):
    t = qkv.shape[0]
    tq = blocks_per_tile * ATT_BLOCK
    nb = seq // tq
    kvw = 2 * ODD_KV
    q_col = ODD_Q // kvw
    prev_block = lambda b, n: jnp.maximum((b * nb + n) * blocks_per_tile - 1, 0)
    return pl.pallas_call(
        _swa_kernel,
        grid=(bsz, nb),
        in_specs=[pl.BlockSpec(memory_space=pltpu.SMEM),
                  pl.BlockSpec((tq, ODD_Q), lambda b, n: (b * nb + n, 0)),
                  pl.BlockSpec((tq, kvw), lambda b, n: (b * nb + n, q_col)),
                  pl.BlockSpec((tq, kvw), lambda b, n: (b * nb + n, q_col + 1)),
                  pl.BlockSpec((ATT_BLOCK, kvw), lambda b, n: (prev_block(b, n), q_col)),
                  pl.BlockSpec((ATT_BLOCK, kvw), lambda b, n: (prev_block(b, n), q_col + 1))],
        out_specs=pl.BlockSpec((tq, ODD_Q), lambda b, n: (b * nb + n, 0)),
        out_shape=jax.ShapeDtypeStruct((t, ODD_Q), BF16),
        compiler_params=_cparams("parallel", "arbitrary"),
        name="swa_sink_attention",
    )(sinks, qkv, qkv, qkv, qkv, qkv)


def _rope_table_kernel(pos_ref, freq_ref, sign_ref, cos_ref, sin_ref):
    ang = pos_ref[...] * freq_ref[...]
    cos_ref[...] = jnp.cos(ang)
    sin_ref[...] = jnp.sin(ang) * sign_ref[...]


def _rope_tables(positions, tm=1024):
    t = positions.size
    half = SWA_HEAD_DIM // 2
    inv_freq = ROPE_THETA ** (-jnp.arange(half, dtype=F32) / half)
    lane = jnp.arange(LANES)
    freq = inv_freq[lane % half].reshape(1, LANES)
    sign = jnp.where((lane % SWA_HEAD_DIM) < half, -1.0, 1.0).astype(F32).reshape(1, LANES)
    pos = positions.astype(F32).reshape(t, 1)
    row = pl.BlockSpec((tm, LANES), lambda i: (i, 0))
    const = pl.BlockSpec((1, LANES), lambda i: (0, 0))
    return pl.pallas_call(
        _rope_table_kernel,
        grid=(t // tm,),
        in_specs=[pl.BlockSpec((tm, 1), lambda i: (i, 0)), const, const],
        out_specs=[row, row],
        out_shape=[jax.ShapeDtypeStruct((t, LANES), F32)] * 2,
        compiler_params=_cparams("parallel"),
        name="rope_tables",
    )(pos, freq, sign)


def _even_mixer(x, mix_gain, bsz, seq, layer, w_in, w_qkv, conv_w, conv_b, dt_bias, a_log, d_skip, gate_norm,
                w_out):
    zx_end = SSD_D_INNER + SSD_XBC
    dils = tuple(dilation for _, dilation in DIL_PATTERNS)
    h, grouped = _rmsnorm_grouped(x, mix_gain, bsz, seq, [d for d in dils if d != 1])
    grouped = iter(grouped)
    hs = [h if d == 1 else next(grouped) for d in dils]

    zx, dt_raw = _matmul_with_dt(h, w_in, layer, zx_end)
    y_a = _ssd_branch(zx, dt_raw, conv_w, conv_b, dt_bias, a_log, d_skip, gate_norm, bsz, seq)

    n_groups = len(DIL_PATTERNS)
    tn = 1024
    per_part = DIL_WIDTH // tn
    outs, lses = [], []
    for g, (window, dilation) in enumerate(DIL_PATTERNS):
        col_block = lambda j, g=g: ((j // per_part) * n_groups + g) * per_part + j % per_part
        qkv = _matmul(hs[g], w_qkv, BF16, tn=tn, n_cols=3 * DIL_WIDTH, col_block=col_block, layer=layer,
                      name="dilated_qkv_proj")
        o, lse = _dilated_group_attention(qkv, bsz, seq, dilation, window // dilation)
        outs.append(o)
        lses.append(lse)
    y_b = _combine_groups(outs, lses, bsz, seq, dils)
    return _matmul_residual([y_a, y_b], w_out, layer, x)


def _odd_mixer(x, h, bsz, seq, layer, cos, sin, w_in, b_in, sinks, w_out):
    def doubled(cols):
        c = cols.reshape(cols.shape[:-1] + (SWA_KV_HEADS, SWA_HEAD_DIM))
        return jnp.concatenate([c, c], axis=-1).reshape(cols.shape[:-1] + (2 * ODD_KV,))

    w = jnp.concatenate([w_in[:, :ODD_Q], doubled(w_in[:, ODD_Q:ODD_Q + ODD_KV]),
                         doubled(w_in[:, ODD_Q + ODD_KV:])], axis=1).astype(BF16)
    b = jnp.concatenate([b_in[:ODD_Q], doubled(b_in[ODD_Q:ODD_Q + ODD_KV]), doubled(b_in[ODD_Q + ODD_KV:])])
    qkv = _matmul_bias_rope(h, w, b, cos, sin, rope_cols=ODD_Q + 2 * ODD_KV)
    o = _swa_attention(qkv, sinks, bsz, seq)
    return _matmul_residual([o], w_out, layer, x)


def kernel(x, mem, positions, ffn1_norm, ffn1_w1, ffn1_w2, mix_norm, even_w_in, even_conv_w, even_conv_b,
           even_dt_bias, even_a_log, even_d_skip, even_gate_norm, even_w_out, odd_w_in, odd_b_in, odd_sinks,
           odd_w_out, mem_norm, mem_w_kv, xa_norm, xa_w_q, xa_w_o, ffn2_norm, ffn2_w1, ffn2_w2, final_norm):
    bsz, seq, d = x.shape
    depth = ffn1_norm.shape[0]
    bf = lambda a: a.astype(BF16)
    x = x.reshape(bsz * seq, d)

    ffn1_w1, ffn1_w2, ffn2_w1, ffn2_w2 = bf(ffn1_w1), bf(ffn1_w2), bf(ffn2_w1), bf(ffn2_w2)
    xa_w_q, xa_w_o = bf(xa_w_q), bf(xa_w_o)
    even_qkv = bf(even_w_in[:, :, SSD_D_INNER + SSD_XBC + SSD_HEADS:])
    even_w_in, even_w_out, odd_w_out = bf(even_w_in), bf(even_w_out), bf(odd_w_out)

    mem_h = _rmsnorm(mem.reshape(bsz * MEM_LEN, d), mem_norm, BF16)
    mem_kv = _matmul(mem_h, bf(mem_w_kv), BF16, name="memory_kv_proj")
    cos, sin = _rope_tables(positions)

    for i in range(depth):
        j = i // 2
        x = _ffn(x, ffn1_norm[i], ffn1_w1, ffn1_w2, i)
        if i % 2 == 0:
            x = _even_mixer(x, mix_norm[i], bsz, seq, j, even_w_in, even_qkv, even_conv_w[j], even_conv_b[j],
                            even_dt_bias[j], even_a_log[j], even_d_skip[j], even_gate_norm[j], even_w_out)
        else:
            h = _rmsnorm(x, mix_norm[i], BF16)
            x = _odd_mixer(x, h, bsz, seq, j, cos, sin, odd_w_in[j], odd_b_in[j], odd_sinks[j], odd_w_out)
        x = _cross_attention(x, xa_norm[i], xa_w_q, mem_kv, xa_w_o, i, seq)
        x = _ffn(x, ffn2_norm[i], ffn2_w1, ffn2_w2, i, final_gain=final_norm if i == depth - 1 else None)
    return x.reshape(bsz, seq, d)
```
